```python
import math
import jax, jax.numpy as jnp
from jax import lax
import numpy as np

D_MODEL = 1024
BATCH = 8
SEQ = 4096
DEPTH = 4

CONV_DIM = 512
CONV_WIDTH = 31
MLA_HEADS = 8
MLA_NOPE = 64
MLA_ROPE = 32
MLA_V = 64
MLA_Q_RANK = 256
MLA_KV_RANK = 128
ROPE_THETA = 10000.0
DIFF_HEADS = 4
DIFF_HEAD = 64
DIFF_V = 2 * DIFF_HEAD
REL_BUCKETS = 32
REL_MAX_DIST = 128
D_FF = 2816
FFN_CONV = 3
N_BRANCH = 3
Q_BLOCK = 128
EPS = 1e-6

W_GLU = 2 * CONV_DIM
W_DQ = DIFF_HEADS * 2 * DIFF_HEAD
W_DK = DIFF_HEADS * 2 * DIFF_HEAD
W_DV = DIFF_HEADS * DIFF_V
W_GATE = N_BRANCH * D_MODEL
D_IN = W_GLU + MLA_Q_RANK + MLA_KV_RANK + MLA_ROPE + W_DQ + W_DK + W_DV + W_GATE

kernel_name = 'hybrid_conv_mla_diffattn_gated_trunk'


def rmsnorm(x, g):
    x32 = x.astype(jnp.float32)
    y = x32 * lax.rsqrt(jnp.mean(x32 * x32, axis=-1, keepdims=True) + EPS)
    return (y * g.astype(jnp.float32)).astype(x.dtype)


def layernorm(x, g, b):
    x32 = x.astype(jnp.float32)
    mu = jnp.mean(x32, axis=-1, keepdims=True)
    var = jnp.mean(jnp.square(x32 - mu), axis=-1, keepdims=True)
    y = (x32 - mu) * lax.rsqrt(var + EPS)
    return (y * g.astype(jnp.float32) + b.astype(jnp.float32)).astype(x.dtype)


def causal_dwconv(x, w, b):
    k = w.shape[0]
    y = lax.conv_general_dilated(
        x, w[:, None, :].astype(x.dtype), window_strides=(1,), padding=[(k - 1, 0)],
        dimension_numbers=('NWC', 'WIO', 'NWC'), feature_group_count=x.shape[-1])
    return y + b.astype(x.dtype)


def rope(x, positions):
    half = x.shape[-1] // 2
    freqs = ROPE_THETA ** (-jnp.arange(half, dtype=jnp.float32) / half)
    ang = positions.astype(jnp.float32)[..., None] * freqs
    ang = ang.reshape(ang.shape[:2] + (1,) * (x.ndim - 3) + (half,))
    cos = jnp.cos(ang).astype(x.dtype)
    sin = jnp.sin(ang).astype(x.dtype)
    x1, x2 = x[..., :half], x[..., half:]
    return jnp.concatenate([x1 * cos - x2 * sin, x2 * cos + x1 * sin], axis=-1)


def rel_bucket(n):
    n = jnp.maximum(n, 0)
    max_exact = REL_BUCKETS // 2
    nf = jnp.maximum(n, 1).astype(jnp.float32)
    large = max_exact + (jnp.log(nf / max_exact) / math.log(REL_MAX_DIST / max_exact)
                         * (REL_BUCKETS - max_exact)).astype(jnp.int32)
    large = jnp.minimum(large, REL_BUCKETS - 1)
    return jnp.where(n < max_exact, n, large)


def to_blocks(t):
    b, s = t.shape[:2]
    t = t.reshape((b, s // Q_BLOCK, Q_BLOCK) + t.shape[2:])
    return jnp.moveaxis(t, 1, 0)


def from_blocks(t):
    t = jnp.moveaxis(t, 0, 1)
    return t.reshape((t.shape[0], t.shape[1] * t.shape[2]) + t.shape[3:])


def mla_attention(c_q, c_kv, k_rope, positions, g_q, w_uq, g_kv, w_ukv):
    b, s, _ = c_q.shape
    q = (rmsnorm(c_q, g_q) @ w_uq).reshape(b, s, MLA_HEADS, MLA_NOPE + MLA_ROPE)
    q_nope, q_pe = q[..., :MLA_NOPE], rope(q[..., MLA_NOPE:], positions)
    kv = (rmsnorm(c_kv, g_kv) @ w_ukv).reshape(b, s, MLA_HEADS, MLA_NOPE + MLA_V)
    k_nope, v = kv[..., :MLA_NOPE], kv[..., MLA_NOPE:]
    k_pe = rope(k_rope, positions)
    scale = (MLA_NOPE + MLA_ROPE) ** -0.5
    kidx = jnp.arange(s)
    qidx = kidx.reshape(s // Q_BLOCK, Q_BLOCK)

    def block(args):
        qn_b, qp_b, qi = args
        sc = (jnp.einsum('bqhd,bkhd->bhqk', qn_b, k_nope)
              + jnp.einsum('bqhr,bkr->bhqk', qp_b, k_pe)).astype(jnp.float32) * scale
        sc = jnp.where(kidx[None, :] <= qi[:, None], sc, -jnp.inf)
        p = jax.nn.softmax(sc, axis=-1).astype(v.dtype)
        return jnp.einsum('bhqk,bkhd->bqhd', p, v)

    o = lax.map(block, (to_blocks(q_nope), to_blocks(q_pe), qidx))
    return from_blocks(o).reshape(b, s, MLA_HEADS * MLA_V)


def diff_attention(q, k, v, positions, rel_table, lq1, lk1, lq2, lk2, g_sub, lam_init):
    b, s, _ = q.shape
    q = q.reshape(b, s, DIFF_HEADS, 2, DIFF_HEAD)
    k = k.reshape(b, s, DIFF_HEADS, 2, DIFF_HEAD)
    v = v.reshape(b, s, DIFF_HEADS, DIFF_V)
    f32 = jnp.float32
    lam = (jnp.exp(jnp.sum(lq1.astype(f32) * lk1.astype(f32)))
           - jnp.exp(jnp.sum(lq2.astype(f32) * lk2.astype(f32))) + lam_init)
    scale = DIFF_HEAD ** -0.5
    kidx = jnp.arange(s)
    qidx = kidx.reshape(s // Q_BLOCK, Q_BLOCK)

    def block(args):
        q_b, pos_b, qi = args
        sc = jnp.einsum('bqhmd,bkhmd->mbhqk', q_b, k).astype(f32) * scale
        bucket = rel_bucket(pos_b[:, :, None] - positions[:, None, :])
        bias = jnp.transpose(rel_table[bucket], (0, 3, 1, 2)).astype(f32)
        sc = jnp.where(kidx[None, :] <= qi[:, None], sc + bias[None], -jnp.inf)
        p = jax.nn.softmax(sc, axis=-1)
        a = (p[0] - lam * p[1]).astype(v.dtype)
        return jnp.einsum('bhqk,bkhd->bqhd', a, v)

    o = from_blocks(lax.map(block, (to_blocks(q), to_blocks(positions), qidx)))
    o = rmsnorm(o, g_sub) * (1.0 - lam_init)
    return o.reshape(b, s, DIFF_HEADS * DIFF_V)


def setup_inputs(seed: int = 0) -> dict:
    key = jax.random.key(seed)
    ks = jax.random.split(key, 40)
    f32 = jnp.float32

    def nrm(k, shape, scale):
        return jax.random.normal(k, shape, f32) * scale

    def gain(k, shape):
        return 1.0 + 0.02 * jax.random.normal(k, shape, f32)

    L = DEPTH
    x = nrm(ks[0], (BATCH, SEQ, D_MODEL), 1.0)
    offs = jax.random.randint(ks[1], (BATCH, 1), 0, 1024, dtype=jnp.int32)
    positions = offs + jnp.arange(SEQ, dtype=jnp.int32)[None, :]
    return {
        'x': x,
        'positions': positions,
        'rel_bias': nrm(ks[2], (REL_BUCKETS, DIFF_HEADS), 0.5),
        'norm_mix': gain(ks[3], (L, D_MODEL)),
        'w_in': nrm(ks[4], (L, D_MODEL, D_IN), D_MODEL ** -0.5),
        'gate_bias': nrm(ks[5], (L, W_GATE), 0.02),
        'conv_w': nrm(ks[6], (L, CONV_WIDTH, CONV_DIM), CONV_WIDTH ** -0.5),
        'conv_b': nrm(ks[7], (L, CONV_DIM), 0.02),
        'conv_ln_g': gain(ks[8], (L, CONV_DIM)),
        'conv_ln_b': nrm(ks[9], (L, CONV_DIM), 0.02),
        'w_conv_out': nrm(ks[10], (L, CONV_DIM, D_MODEL), CONV_DIM ** -0.5),
        'mla_q_norm': gain(ks[11], (L, MLA_Q_RANK)),
        'w_uq': nrm(ks[12], (L, MLA_Q_RANK, MLA_HEADS * (MLA_NOPE + MLA_ROPE)), MLA_Q_RANK ** -0.5),
        'mla_kv_norm': gain(ks[13], (L, MLA_KV_RANK)),
        'w_ukv': nrm(ks[14], (L, MLA_KV_RANK, MLA_HEADS * (MLA_NOPE + MLA_V)), MLA_KV_RANK ** -0.5),
        'w_mla_out': nrm(ks[15], (L, MLA_HEADS * MLA_V, D_MODEL), (MLA_HEADS * MLA_V) ** -0.5),
        'diff_lam_q1': nrm(ks[16], (L, DIFF_HEAD), 0.1),
        'diff_lam_k1': nrm(ks[17], (L, DIFF_HEAD), 0.1),
        'diff_lam_q2': nrm(ks[18], (L, DIFF_HEAD), 0.1),
        'diff_lam_k2': nrm(ks[19], (L, DIFF_HEAD), 0.1),
        'diff_sub_norm': gain(ks[20], (L, DIFF_V)),
        'w_diff_out': nrm(ks[21], (L, DIFF_HEADS * DIFF_V, D_MODEL), (DIFF_HEADS * DIFF_V) ** -0.5),
        'w_out': nrm(ks[22], (L, D_MODEL, D_MODEL), D_MODEL ** -0.5),
        'norm_ffn': gain(ks[23], (L, D_MODEL)),
        'w_up': nrm(ks[24], (L, D_MODEL, 2 * D_FF), D_MODEL ** -0.5),
        'ffn_conv_w': nrm(ks[25], (L, FFN_CONV, 2 * D_FF), FFN_CONV ** -0.5),
        'ffn_conv_b': nrm(ks[26], (L, 2 * D_FF), 0.02),
        'w_down': nrm(ks[27], (L, D_FF, D_MODEL), D_FF ** -0.5),
        'norm_final': gain(ks[28], (D_MODEL,)),
    }


def reference(x, positions, rel_bias, norm_mix, w_in, gate_bias, conv_w, conv_b, conv_ln_g,
              conv_ln_b, w_conv_out, mla_q_norm, w_uq, mla_kv_norm, w_ukv, w_mla_out,
              diff_lam_q1, diff_lam_k1, diff_lam_q2, diff_lam_k2, diff_sub_norm, w_diff_out,
              w_out, norm_ffn, w_up, ffn_conv_w, ffn_conv_b, w_down, norm_final):
    b, s, d = x.shape
    cuts = list(np.cumsum([W_GLU, MLA_Q_RANK, MLA_KV_RANK, MLA_ROPE, W_DQ, W_DK, W_DV]))
    for l in range(DEPTH):
        h = rmsnorm(x, norm_mix[l])
        z = h @ w_in[l]
        u_glu, c_q, c_kv, k_rope, dq, dk, dv, gates = jnp.split(z, cuts, axis=-1)

        a = u_glu[..., :CONV_DIM] * jax.nn.sigmoid(u_glu[..., CONV_DIM:])
        a = causal_dwconv(a, conv_w[l], conv_b[l])
        a = jax.nn.silu(layernorm(a, conv_ln_g[l], conv_ln_b[l]))
        y_a = a @ w_conv_out[l]

        y_b = mla_attention(c_q, c_kv, k_rope, positions, mla_q_norm[l], w_uq[l],
                            mla_kv_norm[l], w_ukv[l]) @ w_mla_out[l]

        lam_init = 0.8 - 0.6 * math.exp(-0.3 * l)
        y_c = diff_attention(dq, dk, dv, positions, rel_bias, diff_lam_q1[l], diff_lam_k1[l],
                             diff_lam_q2[l], diff_lam_k2[l], diff_sub_norm[l], lam_init) @ w_diff_out[l]

        g = jax.nn.sigmoid(gates + gate_bias[l]).reshape(b, s, N_BRANCH, d)
        merged = g[:, :, 0] * y_a + g[:, :, 1] * y_b + g[:, :, 2] * y_c
        x = x + merged @ w_out[l]

        h = rmsnorm(x, norm_ffn[l])
        u = causal_dwconv(h @ w_up[l], ffn_conv_w[l], ffn_conv_b[l])
        x = x + (jax.nn.silu(u[..., :D_FF]) * u[..., D_FF:]) @ w_down[l]
    return rmsnorm(x, norm_final)
```

```python
import functools
import math

import numpy as np
import jax
import jax.numpy as jnp
from jax import lax
from jax.experimental import pallas as pl
from jax.experimental.pallas import tpu as pltpu

F32 = jnp.float32
BF16 = jnp.bfloat16

D_MODEL = 1024
CONV_DIM = 512
CONV_WIDTH = 31
MLA_HEADS = 8
MLA_NOPE = 64
MLA_ROPE = 32
MLA_V = 64
MLA_Q_RANK = 256
MLA_KV_RANK = 128
ROPE_THETA = 10000.0
DIFF_HEADS = 4
DIFF_HEAD = 64
DIFF_V = 2 * DIFF_HEAD
REL_BUCKETS = 32
REL_MAX_DIST = 128
D_FF = 2816
FFN_CONV = 3
N_BRANCH = 3
EPS = 1e-6

LANES = 128
SUBLANES = 8
VMEM_LIMIT = 56 * 1024 * 1024
NEG_BIG = -1e30

W_GLU = 2 * CONV_DIM
W_DQK = DIFF_HEADS * 2 * DIFF_HEAD
W_DV = DIFF_HEADS * DIFF_V
W_GATE = N_BRANCH * D_MODEL
W_SMALL = MLA_Q_RANK + MLA_KV_RANK + 2 * LANES
FF_CHUNK = 256
N_FF_CHUNK = D_FF // FF_CHUNK
CONV_HALO = 32


def _cparams(*sem):
    return pltpu.CompilerParams(dimension_semantics=sem, vmem_limit_bytes=VMEM_LIMIT)


def _rms(x, g):
    return x * lax.rsqrt(jnp.mean(x * x, axis=-1, keepdims=True) + EPS) * g


def _dot(a, b):
    return jnp.dot(a, b, preferred_element_type=F32)


def _dot_nt(a, b):
    return lax.dot_general(a, b, (((1,), (1,)), ((), ())), preferred_element_type=F32)


def _in_proj_kernel(x_ref, g_ref, w_ref, a_ref, small_ref, dq_ref, dk_ref, dv_ref):
    h = _rms(x_ref[...], g_ref[...]).astype(BF16)
    c0 = 0
    u = _dot(h, w_ref[:, c0:c0 + W_GLU])
    a_ref[...] = u[:, :CONV_DIM] * jax.nn.sigmoid(u[:, CONV_DIM:])
    c0 += W_GLU
    small_ref[...] = _dot(h, w_ref[:, c0:c0 + W_SMALL])
    c0 += W_SMALL
    dq_ref[...] = (_dot(h, w_ref[:, c0:c0 + W_DQK]) * (DIFF_HEAD ** -0.5)).astype(BF16)
    c0 += W_DQK
    dk_ref[...] = _dot(h, w_ref[:, c0:c0 + W_DQK]).astype(BF16)
    c0 += W_DQK
    dv_ref[...] = _dot(h, w_ref[:, c0:c0 + W_DV]).astype(BF16)


def _in_proj(x2, g, w, tm):
    t = x2.shape[0]
    n_w = w.shape[1]
    row = lambda n: pl.BlockSpec((tm, n), lambda i: (i, 0))
    full = lambda a, b: pl.BlockSpec((a, b), lambda i: (0, 0))
    return pl.pallas_call(
        _in_proj_kernel,
        grid=(t // tm,),
        in_specs=[row(D_MODEL), full(1, D_MODEL), full(D_MODEL, n_w)],
        out_specs=[row(CONV_DIM), row(W_SMALL), row(W_DQK), row(W_DQK), row(W_DV)],
        out_shape=[jax.ShapeDtypeStruct((t, CONV_DIM), F32),
                   jax.ShapeDtypeStruct((t, W_SMALL), F32),
                   jax.ShapeDtypeStruct((t, W_DQK), BF16),
                   jax.ShapeDtypeStruct((t, W_DQK), BF16),
                   jax.ShapeDtypeStruct((t, W_DV), BF16)],
        compiler_params=_cparams("parallel"),
        name="in_proj",
    )(x2, g, w)


def _mla_prep_kernel(small_ref, cos_ref, sin_ref, gq_ref, gkv_ref, wqa_ref, wqb_ref, wk_ref, wv_ref,
                     q_ref, k_ref, v_ref):
    sm = small_ref[0]
    cq = sm[:, :MLA_Q_RANK]
    ckv = sm[:, MLA_Q_RANK:MLA_Q_RANK + MLA_KV_RANK]
    kr = sm[:, MLA_Q_RANK + MLA_KV_RANK:MLA_Q_RANK + MLA_KV_RANK + LANES]
    kr_rot = sm[:, MLA_Q_RANK + MLA_KV_RANK + LANES:]
    cos = cos_ref[0]
    sin = sin_ref[0]
    hq = _rms(cq, gq_ref[...]).astype(BF16)
    hkv = _rms(ckv, gkv_ref[...]).astype(BF16)
    qa = _dot(hq, wqa_ref[...])
    qb = _dot(hq, wqb_ref[...])
    kn = _dot(hkv, wk_ref[...])
    vv = _dot(hkv, wv_ref[...])
    kpe = kr * cos + kr_rot * sin
    lane = lax.broadcasted_iota(jnp.int32, (1, LANES), 1)
    scale = (MLA_NOPE + MLA_ROPE) ** -0.5
    for h in range(MLA_HEADS):
        sl = slice(h * LANES, (h + 1) * LANES)
        q_ref[0, h] = ((qa[:, sl] * cos + qb[:, sl] * sin) * scale).astype(BF16)
        k_ref[0, h] = (kn[:, sl] + kpe).astype(BF16)
        one_lane = MLA_V if h % 2 == 0 else 0
        v_ref[0, h] = (vv[:, sl] + (lane == one_lane).astype(F32)).astype(BF16)


def _mla_prep(small3, cos_t, sin_t, gq, gkv, wqa, wqb, wk, wv, tm):
    b, s, _ = small3.shape
    full = lambda a: pl.BlockSpec(a.shape, lambda bi, i: (0,) * a.ndim)
    head_out = pl.BlockSpec((1, MLA_HEADS, tm, LANES), lambda bi, i: (bi, 0, i, 0))
    return pl.pallas_call(
        _mla_prep_kernel,
        grid=(b, s // tm),
        in_specs=[pl.BlockSpec((1, tm, W_SMALL), lambda bi, i: (bi, i, 0)),
                  pl.BlockSpec((1, tm, LANES), lambda bi, i: (bi, i, 0)),
                  pl.BlockSpec((1, tm, LANES), lambda bi, i: (bi, i, 0)),
                  full(gq), full(gkv), full(wqa), full(wqb), full(wk), full(wv)],
        out_specs=[head_out, head_out, head_out],
        out_shape=[jax.ShapeDtypeStruct((b, MLA_HEADS, s, LANES), BF16)] * 3,
        compiler_params=_cparams("parallel", "parallel"),
        name="mla_prep",
    )(small3, cos_t, sin_t, gq, gkv, wqa, wqb, wk, wv)


def _attn_block(q, k, v, bias, state, first, denom_in_acc):
    m_ref, l_ref, acc_ref = state
    s = _dot_nt(q, k)
    if bias is not None:
        rows, tk = s.shape
        s = (s.reshape(rows // bias.shape[0], bias.shape[0], tk) + bias[None]).reshape(rows, tk)
    row_max = jnp.max(s, axis=1, keepdims=True)
    if first:
        m_new = row_max
        p = jnp.exp(s - m_new)
        acc_ref[...] = _dot(p.astype(BF16), v)
        if not denom_in_acc:
            l_ref[...] = jnp.sum(p, axis=1, keepdims=True)
    else:
        m_old = m_ref[...]
        m_new = jnp.maximum(m_old, row_max)
        alpha = jnp.exp(m_old - m_new)
        p = jnp.exp(s - m_new)
        acc_ref[...] = alpha * acc_ref[...] + _dot(p.astype(BF16), v)
        if not denom_in_acc:
            l_ref[...] = alpha * l_ref[...] + jnp.sum(p, axis=1, keepdims=True)
    m_ref[...] = m_new


def _causal_flash(q, k_ref, v_ref, qi, tq, diag_bias, sub_bias, state, denom_in_acc):
    def kv(j):
        rows = pl.ds(pl.multiple_of(j * tq, tq), tq)
        return k_ref[rows, :], v_ref[rows, :]

    k, v = kv(qi)
    _attn_block(q, k, v, diag_bias, state, True, denom_in_acc)

    if sub_bias is not None:
        @pl.when(qi >= 1)
        def _():
            k1, v1 = kv(qi - 1)
            _attn_block(q, k1, v1, sub_bias, state, False, denom_in_acc)
        n_plain = jnp.maximum(qi - 1, 0)
    else:
        n_plain = qi

    def body(j, carry):
        kj, vj = kv(j)
        _attn_block(q, kj, vj, None, state, False, denom_in_acc)
        return carry

    lax.fori_loop(0, n_plain, body, 0)


def _mla_attn_kernel(q_ref, k_ref, v_ref, o_ref, m_ref, acc_ref, *, tq):
    qi = pl.program_id(2)
    row = lax.broadcasted_iota(jnp.int32, (tq, tq), 0)
    col = lax.broadcasted_iota(jnp.int32, (tq, tq), 1)
    mask = jnp.where(row >= col, 0.0, NEG_BIG).astype(F32)
    lane = lax.broadcasted_iota(jnp.int32, (1, LANES), 1)
    outs = []
    for hh in range(2):
        _causal_flash(q_ref[0, hh], k_ref.at[0, hh], v_ref.at[0, hh], qi, tq, mask, None,
                      (m_ref, None, acc_ref), True)
        acc = acc_ref[...]
        denom_lane = MLA_V if hh == 0 else 0
        denom = jnp.sum(jnp.where(lane == denom_lane, acc, 0.0), axis=1, keepdims=True)
        outs.append(acc / denom)
    o_ref[0] = jnp.where(lane < MLA_V, outs[0], outs[1]).astype(BF16)


def _mla_attn(q, k, v, tq):
    b, nh, s, _ = q.shape
    return pl.pallas_call(
        functools.partial(_mla_attn_kernel, tq=tq),
        grid=(b, nh // 2, s // tq),
        in_specs=[pl.BlockSpec((1, 2, tq, LANES), lambda bi, hp, i: (bi, hp, i, 0)),
                  pl.BlockSpec((1, 2, s, LANES), lambda bi, hp, i: (bi, hp, 0, 0)),
                  pl.BlockSpec((1, 2, s, LANES), lambda bi, hp, i: (bi, hp, 0, 0))],
        out_specs=pl.BlockSpec((1, tq, LANES), lambda bi, hp, i: (bi, i, hp)),
        out_shape=jax.ShapeDtypeStruct((b, s, nh * MLA_V), BF16),
        scratch_shapes=[pltpu.VMEM((tq, 1), F32), pltpu.VMEM((tq, LANES), F32)],
        compiler_params=_cparams("parallel", "parallel", "arbitrary"),
        name="mla_attn",
    )(q, k, v)


def _bucket_upper_bounds():
    n = np.arange(0, 4 * REL_MAX_DIST, dtype=np.int32)
    max_exact = REL_BUCKETS // 2
    nf = np.maximum(n, 1).astype(np.float32)
    large = max_exact + (np.log(nf / np.float32(max_exact)) / np.float32(math.log(REL_MAX_DIST / max_exact))
                         * np.float32(REL_BUCKETS - max_exact)).astype(np.int32)
    large = np.minimum(large, REL_BUCKETS - 1)
    bucket = np.where(n < max_exact, n, large)
    assert np.all(np.diff(bucket) >= 0) and bucket[-1] == REL_BUCKETS - 1
    return [int(np.max(n[bucket == b])) for b in range(REL_BUCKETS - 1)]


def _bias_tile_kernel(tbl_ref, o_ref, *, tq, bounds):
    h = pl.program_id(0)
    kind = pl.program_id(1)
    row = lax.broadcasted_iota(jnp.int32, (tq, tq), 0)
    col = lax.broadcasted_iota(jnp.int32, (tq, tq), 1)
    n = row - col + kind * tq
    far = tbl_ref[(REL_BUCKETS - 1) * DIFF_HEADS + h]
    r = jnp.zeros((tq, tq), F32)
    for bkt in range(REL_BUCKETS - 2, -1, -1):
        r = jnp.where(n <= bounds[bkt], tbl_ref[bkt * DIFF_HEADS + h] - far, r)
    o_ref[0, 0] = jnp.where(n >= 0, r, NEG_BIG)


def _bias_tiles(rel_bias, tq):
    bounds = _bucket_upper_bounds()
    assert bounds[-1] < tq, "bias must be constant beyond the sub-diagonal key block"
    return pl.pallas_call(
        functools.partial(_bias_tile_kernel, tq=tq, bounds=bounds),
        grid=(DIFF_HEADS, 2),
        in_specs=[pl.BlockSpec(memory_space=pltpu.SMEM)],
        out_specs=pl.BlockSpec((1, 1, tq, tq), lambda h, kd: (h, kd, 0, 0)),
        out_shape=jax.ShapeDtypeStruct((DIFF_HEADS, 2, tq, tq), F32),
        compiler_params=_cparams("parallel", "parallel"),
        name="rel_bias_tiles",
    )(rel_bias.reshape(-1))


def _diff_attn_kernel(q_ref, k_ref, v_ref, bias_ref, lq1_ref, lk1_ref, lq2_ref, lk2_ref, gsub_ref,
                      o_ref, m_ref, l_ref, acc_ref, *, tq, lam_init):
    qi = pl.program_id(2)
    lam = (jnp.exp(jnp.sum(lq1_ref[...] * lk1_ref[...], axis=1, keepdims=True))
           - jnp.exp(jnp.sum(lq2_ref[...] * lk2_ref[...], axis=1, keepdims=True)) + lam_init)
    q = q_ref[0]
    lane = lax.broadcasted_iota(jnp.int32, (1, LANES), 1)
    zero = jnp.zeros_like(q)
    q2 = jnp.concatenate([jnp.where(lane < DIFF_HEAD, q, zero), jnp.where(lane >= DIFF_HEAD, q, zero)], axis=0)
    _causal_flash(q2, k_ref.at[0], v_ref.at[0], qi, tq, bias_ref[0, 0], bias_ref[0, 1],
                  (m_ref, l_ref, acc_ref), False)
    o = acc_ref[...] / l_ref[...]
    o = o[:tq] - lam * o[tq:]
    o_ref[0] = (_rms(o, gsub_ref[...]) * (1.0 - lam_init)).astype(BF16)


def _diff_attn(dq, dk, dv, bias_tiles, lq1, lk1, lq2, lk2, gsub, lam_init, tq):
    b, s, _ = dq.shape
    vec = lambda a: pl.BlockSpec(a.shape, lambda bi, h, i: (0, 0))
    return pl.pallas_call(
        functools.partial(_diff_attn_kernel, tq=tq, lam_init=lam_init),
        grid=(b, DIFF_HEADS, s // tq),
        in_specs=[pl.BlockSpec((1, tq, LANES), lambda bi, h, i: (bi, i, h)),
                  pl.BlockSpec((1, s, LANES), lambda bi, h, i: (bi, 0, h)),
                  pl.BlockSpec((1, s, LANES), lambda bi, h, i: (bi, 0, h)),
                  pl.BlockSpec((1, 2, tq, tq), lambda bi, h, i: (h, 0, 0, 0)),
                  vec(lq1), vec(lk1), vec(lq2), vec(lk2), vec(gsub)],
        out_specs=pl.BlockSpec((1, tq, LANES), lambda bi, h, i: (bi, i, h)),
        out_shape=jax.ShapeDtypeStruct((b, s, W_DV), BF16),
        scratch_shapes=[pltpu.VMEM((2 * tq, 1), F32), pltpu.VMEM((2 * tq, 1), F32),
                        pltpu.VMEM((2 * tq, LANES), F32)],
        compiler_params=_cparams("parallel", "parallel", "arbitrary"),
        name="diff_attn",
    )(dq, dk, dv, bias_tiles, lq1, lk1, lq2, lk2, gsub)


def _conv_kernel(a_ref, halo_ref, w_ref, b_ref, lng_ref, lnb_ref, o_ref, buf_ref, *, tm):
    i = pl.program_id(1)
    halo = halo_ref[0]
    buf_ref[0:CONV_HALO, :] = jnp.where(i == 0, jnp.zeros_like(halo), halo)
    buf_ref[CONV_HALO:, :] = a_ref[0]
    base = CONV_HALO - (CONV_WIDTH - 1)
    y = jnp.zeros((tm, CONV_DIM), F32) + b_ref[...]
    for j in range(CONV_WIDTH):
        y = y + w_ref[j:j + 1, :] * buf_ref[base + j:base + j + tm, :]
    mu = jnp.mean(y, axis=-1, keepdims=True)
    var = jnp.mean(jnp.square(y - mu), axis=-1, keepdims=True)
    z = (y - mu) * lax.rsqrt(var + EPS) * lng_ref[...] + lnb_ref[...]
    o_ref[0] = (z * jax.nn.sigmoid(z)).astype(BF16)


def _conv_branch(a3, w, bias, lng, lnb, tm):
    b, s, _ = a3.shape
    per = tm // CONV_HALO
    full = lambda a: pl.BlockSpec(a.shape, lambda bi, i: (0, 0))
    return pl.pallas_call(
        functools.partial(_conv_kernel, tm=tm),
        grid=(b, s // tm),
        in_specs=[pl.BlockSpec((1, tm, CONV_DIM), lambda bi, i: (bi, i, 0)),
                  pl.BlockSpec((1, CONV_HALO, CONV_DIM), lambda bi, i: (bi, jnp.maximum(i * per - 1, 0), 0)),
                  full(w), full(bias), full(lng), full(lnb)],
        out_specs=pl.BlockSpec((1, tm, CONV_DIM), lambda bi, i: (bi, i, 0)),
        out_shape=jax.ShapeDtypeStruct((b, s, CONV_DIM), BF16),
        scratch_shapes=[pltpu.VMEM((tm + CONV_HALO, CONV_DIM), F32)],
        compiler_params=_cparams("parallel", "parallel"),
        name="conv_branch",
    )(a3, a3, w, bias, lng, lnb)


def _merge_kernel(x_ref, g_ref, wg_ref, bg_ref, ya_ref, yb_ref, yc_ref, wa_ref, wb_ref, wc_ref, wo_ref, o_ref):
    x = x_ref[...]
    h = _rms(x, g_ref[...]).astype(BF16)
    merged = None
    for br, (y_ref, w_ref) in enumerate(((ya_ref, wa_ref), (yb_ref, wb_ref), (yc_ref, wc_ref))):
        sl = slice(br * D_MODEL, (br + 1) * D_MODEL)
        gate = jax.nn.sigmoid(_dot(h, wg_ref[:, sl]) + bg_ref[:, sl])
        term = gate * _dot(y_ref[...], w_ref[...])
        merged = term if merged is None else merged + term
    o_ref[...] = x + _dot(merged.astype(BF16), wo_ref[...])


def _merge(x2, g, wg, bg, ya, yb, yc, wa, wb, wc, wo, tm):
    t = x2.shape[0]
    row = lambda n: pl.BlockSpec((tm, n), lambda i: (i, 0))
    full = lambda a: pl.BlockSpec(a.shape, lambda i: (0, 0))
    return pl.pallas_call(
        _merge_kernel,
        grid=(t // tm,),
        in_specs=[row(D_MODEL), full(g), full(wg), full(bg), row(CONV_DIM), row(MLA_HEADS * MLA_V), row(W_DV),
                  full(wa), full(wb), full(wc), full(wo)],
        out_specs=row(D_MODEL),
        out_shape=jax.ShapeDtypeStruct((t, D_MODEL), F32),
        compiler_params=_cparams("parallel"),
        name="merge",
    )(x2, g, wg, bg, ya, yb, yc, wa, wb, wc, wo)


def _ffn_kernel(x_ref, g_ref, wv_ref, wg_ref, cw_ref, cb_ref, wd_ref, gf_ref, o_ref,
                h_ref, acc_ref, buf_ref, carry_ref, *, tm, tiles_per_seq, final_norm):
    i = pl.program_id(0)
    x = x_ref[...]
    h_ref[...] = _rms(x, g_ref[...]).astype(BF16)
    acc_ref[...] = jnp.zeros_like(acc_ref)
    pad = SUBLANES

    @pl.when((i % tiles_per_seq) == 0)
    def _():
        carry_ref[...] = jnp.zeros_like(carry_ref)

    def conv3(u0, c, half):
        buf_ref[0:pad, :] = carry_ref[c, half]
        buf_ref[pad:, :] = u0
        carry_ref[c, half] = u0[tm - pad:, :]
        w = cw_ref[c, half]
        return (w[0:1, :] * buf_ref[pad - 2:pad - 2 + tm, :] + w[1:2, :] * buf_ref[pad - 1:pad - 1 + tm, :]
                + w[2:3, :] * u0 + cb_ref[c, half])

    def chunk(c, carry):
        h = h_ref[...]
        uv = conv3(_dot(h, wv_ref[c]), c, 0)
        ug = conv3(_dot(h, wg_ref[c]), c, 1)
        act = (uv * jax.nn.sigmoid(uv) * ug).astype(BF16)
        acc_ref[...] += _dot(act, wd_ref[c])
        return carry

    lax.fori_loop(0, N_FF_CHUNK, chunk, 0)
    y = x + acc_ref[...]
    if final_norm:
        y = _rms(y, gf_ref[...])
    o_ref[...] = y


def _ffn(x2, g, wv, wg, cw, cb, wd, gf, tm, tiles_per_seq, final_norm):
    t = x2.shape[0]
    row = pl.BlockSpec((tm, D_MODEL), lambda i: (i, 0))
    full = lambda a: pl.BlockSpec(a.shape, lambda i: (0,) * a.ndim)
    return pl.pallas_call(
        functools.partial(_ffn_kernel, tm=tm, tiles_per_seq=tiles_per_seq, final_norm=final_norm),
        grid=(t // tm,),
        in_specs=[row, full(g), full(wv), full(wg), full(cw), full(cb), full(wd), full(gf)],
        out_specs=row,
        out_shape=jax.ShapeDtypeStruct((t, D_MODEL), F32),
        scratch_shapes=[pltpu.VMEM((tm, D_MODEL), BF16), pltpu.VMEM((tm, D_MODEL), F32),
                        pltpu.VMEM((tm + SUBLANES, FF_CHUNK), F32),
                        pltpu.VMEM((N_FF_CHUNK, 2, SUBLANES, FF_CHUNK), F32)],
        compiler_params=_cparams("arbitrary"),
        name="ffn",
    )(x2, g, wv, wg, cw, cb, wd, gf)


def _rot_cols(w):
    half = w.shape[-1] // 2
    return jnp.concatenate([-w[..., half:], w[..., :half]], axis=-1)


def _prep_in_proj_weight(w_in_l):
    cuts = np.cumsum([W_GLU, MLA_Q_RANK, MLA_KV_RANK, MLA_ROPE, W_DQK, W_DQK, W_DV])
    glu, cq, ckv, kr, dq, dk, dv, gates = jnp.split(w_in_l, cuts, axis=1)
    z = lambda n: jnp.zeros((D_MODEL, n), w_in_l.dtype)
    tail = LANES - MLA_NOPE - MLA_ROPE
    w = jnp.concatenate([glu, cq, ckv, z(MLA_NOPE), kr, z(tail), z(MLA_NOPE), _rot_cols(kr), z(tail), dq, dk, dv],
                        axis=1)
    return w.astype(BF16), gates.astype(BF16)


def _prep_mla_weights(w_uq_l, w_ukv_l):
    dqk = MLA_NOPE + MLA_ROPE
    wq = w_uq_l.reshape(MLA_Q_RANK, MLA_HEADS, dqk)
    pad = jnp.zeros((MLA_Q_RANK, MLA_HEADS, LANES - dqk), wq.dtype)
    wqa = jnp.concatenate([wq, pad], axis=-1)
    wqb = jnp.concatenate([jnp.zeros_like(wq[..., :MLA_NOPE]), _rot_cols(wq[..., MLA_NOPE:]), pad], axis=-1)
    wkv = w_ukv_l.reshape(MLA_KV_RANK, MLA_HEADS, MLA_NOPE + MLA_V)
    zk = jnp.zeros((MLA_KV_RANK, MLA_HEADS, LANES - MLA_NOPE), wkv.dtype)
    wk = jnp.concatenate([wkv[..., :MLA_NOPE], zk], axis=-1)
    zv = jnp.zeros((MLA_KV_RANK, MLA_HEADS, LANES - MLA_V), wkv.dtype)
    v_even = jnp.concatenate([wkv[..., MLA_NOPE:], zv], axis=-1)
    v_odd = jnp.concatenate([zv, wkv[..., MLA_NOPE:]], axis=-1)
    odd = (jnp.arange(MLA_HEADS) % 2 == 1)[None, :, None]
    wv = jnp.where(odd, v_odd, v_even)
    flat = lambda a: a.reshape(a.shape[0], MLA_HEADS * LANES).astype(BF16)
    return flat(wqa), flat(wqb), flat(wk), flat(wv)


def _rope_tables(positions):
    half = MLA_ROPE // 2
    freqs = ROPE_THETA ** (-jnp.arange(half, dtype=F32) / half)
    ang = positions.astype(F32)[..., None] * freqs
    cos, sin = jnp.cos(ang), jnp.sin(ang)
    b, s = positions.shape
    ones = jnp.ones((b, s, MLA_NOPE), F32)
    zeros_n = jnp.zeros((b, s, MLA_NOPE), F32)
    zeros_t = jnp.zeros((b, s, LANES - MLA_NOPE - MLA_ROPE), F32)
    cos_t = jnp.concatenate([ones, cos, cos, zeros_t], axis=-1)
    sin_t = jnp.concatenate([zeros_n, sin, sin, zeros_t], axis=-1)
    return cos_t, sin_t


def _chunk_cols(w, n):
    return jnp.moveaxis(w.reshape(w.shape[0], n, FF_CHUNK), 1, 0)


def kernel(x, positions, rel_bias, norm_mix, w_in, gate_bias, conv_w, conv_b, conv_ln_g, conv_ln_b, w_conv_out,
           mla_q_norm, w_uq, mla_kv_norm, w_ukv, w_mla_out, diff_lam_q1, diff_lam_k1, diff_lam_q2, diff_lam_k2,
           diff_sub_norm, w_diff_out, w_out, norm_ffn, w_up, ffn_conv_w, ffn_conv_b, w_down, norm_final):
    b, s, d = x.shape
    depth = w_in.shape[0]
    assert d == D_MODEL
    tm = min(512, s)
    tq = min(512, s)
    assert s % tm == 0 and s % tq == 0 and tm % CONV_HALO == 0
    t = b * s
    row2 = lambda a: a.reshape(1, -1)

    cos_t, sin_t = _rope_tables(positions)
    bias_tiles = _bias_tiles(rel_bias, tq)
    x2 = x.reshape(t, d)

    for l in range(depth):
        w_in_p, w_gate = _prep_in_proj_weight(w_in[l])
        a, small, dq, dk, dv = _in_proj(x2, row2(norm_mix[l]), w_in_p, tm)

        y_a = _conv_branch(a.reshape(b, s, CONV_DIM), conv_w[l], row2(conv_b[l]), row2(conv_ln_g[l]),
                           row2(conv_ln_b[l]), tm)

        wqa, wqb, wk, wv = _prep_mla_weights(w_uq[l], w_ukv[l])
        q, k, v = _mla_prep(small.reshape(b, s, W_SMALL), cos_t, sin_t, row2(mla_q_norm[l]), row2(mla_kv_norm[l]),
                            wqa, wqb, wk, wv, tm)
        y_b = _mla_attn(q, k, v, tq)

        lam_init = 0.8 - 0.6 * math.exp(-0.3 * l)
        y_c = _diff_attn(dq.reshape(b, s, W_DQK), dk.reshape(b, s, W_DQK), dv.reshape(b, s, W_DV), bias_tiles,
                         row2(diff_lam_q1[l]), row2(diff_lam_k1[l]), row2(diff_lam_q2[l]), row2(diff_lam_k2[l]),
                         row2(diff_sub_norm[l]), lam_init, tq)

        x2 = _merge(x2, row2(norm_mix[l]), w_gate, row2(gate_bias[l]),
                    y_a.reshape(t, CONV_DIM), y_b.reshape(t, MLA_HEADS * MLA_V), y_c.reshape(t, W_DV),
                    w_conv_out[l].astype(BF16), w_mla_out[l].astype(BF16), w_diff_out[l].astype(BF16),
                    w_out[l].astype(BF16), tm)

        w_up_l = w_up[l].astype(BF16)
        cw = ffn_conv_w[l].reshape(FFN_CONV, 2, N_FF_CHUNK, FF_CHUNK).transpose(2, 1, 0, 3)
        cb = ffn_conv_b[l].reshape(2, N_FF_CHUNK, 1, FF_CHUNK).transpose(1, 0, 2, 3)
        x2 = _ffn(x2, row2(norm_ffn[l]), _chunk_cols(w_up_l[:, :D_FF], N_FF_CHUNK),
                  _chunk_cols(w_up_l[:, D_FF:], N_FF_CHUNK), cw, cb,
                  w_down[l].astype(BF16).reshape(N_FF_CHUNK, FF_CHUNK, D_MODEL), row2(norm_final),
                  tm, s // tm, l == depth - 1)
    return x2.reshape(b, s, d)
```

```python
import functools
import math

import numpy as np
import jax
import jax.numpy as jnp
from jax import lax
from jax.experimental import pallas as pl
from jax.experimental.pallas import tpu as pltpu

F32 = jnp.float32
BF16 = jnp.bfloat16

D_MODEL = 1024
CONV_DIM = 512
CONV_WIDTH = 31
MLA_HEADS = 8
MLA_NOPE = 64
MLA_ROPE = 32
MLA_V = 64
MLA_Q_RANK = 256
MLA_KV_RANK = 128
ROPE_THETA = 10000.0
DIFF_HEADS = 4
DIFF_HEAD = 64
DIFF_V = 2 * DIFF_HEAD
REL_BUCKETS = 32
REL_MAX_DIST = 128
D_FF = 2816
FFN_CONV = 3
N_BRANCH = 3
EPS = 1e-6

LANES = 128
SUBLANES = 8
VMEM_LIMIT = 56 * 1024 * 1024
NEG_BIG = -1e30
LOG2E = math.log2(math.e)

W_GLU = 2 * CONV_DIM
W_DQK = DIFF_HEADS * 2 * DIFF_HEAD
W_DV = DIFF_HEADS * DIFF_V
W_GATE = N_BRANCH * D_MODEL
W_SMALL = MLA_Q_RANK + MLA_KV_RANK + 2 * LANES
FF_CHUNK = 256
N_FF_CHUNK = D_FF // FF_CHUNK
CONV_HALO = 32


def _cparams(*sem):
    return pltpu.CompilerParams(dimension_semantics=sem, vmem_limit_bytes=VMEM_LIMIT)


def _rms(x, g):
    return x * lax.rsqrt(jnp.mean(x * x, axis=-1, keepdims=True) + EPS) * g


def _dot(a, b):
    return jnp.dot(a, b, preferred_element_type=F32)


def _dot_nt(a, b):
    return lax.dot_general(a, b, (((1,), (1,)), ((), ())), preferred_element_type=F32)


def _in_proj_kernel(x_ref, g_ref, w_ref, a_ref, small_ref, dq_ref, dk_ref, dv_ref):
    h = _rms(x_ref[...], g_ref[...]).astype(BF16)
    c0 = 0
    u = _dot(h, w_ref[:, c0:c0 + W_GLU])
    a_ref[...] = u[:, :CONV_DIM] * jax.nn.sigmoid(u[:, CONV_DIM:])
    c0 += W_GLU
    small_ref[...] = _dot(h, w_ref[:, c0:c0 + W_SMALL])
    c0 += W_SMALL
    dq_ref[...] = (_dot(h, w_ref[:, c0:c0 + W_DQK]) * (DIFF_HEAD ** -0.5 * LOG2E)).astype(BF16)
    c0 += W_DQK
    dk_ref[...] = _dot(h, w_ref[:, c0:c0 + W_DQK]).astype(BF16)
    c0 += W_DQK
    dv_ref[...] = _dot(h, w_ref[:, c0:c0 + W_DV]).astype(BF16)


def _in_proj(x2, g, w, tm):
    t = x2.shape[0]
    n_w = w.shape[1]
    row = lambda n: pl.BlockSpec((tm, n), lambda i: (i, 0))
    full = lambda a, b: pl.BlockSpec((a, b), lambda i: (0, 0))
    return pl.pallas_call(
        _in_proj_kernel,
        grid=(t // tm,),
        in_specs=[row(D_MODEL), full(1, D_MODEL), full(D_MODEL, n_w)],
        out_specs=[row(CONV_DIM), row(W_SMALL), row(W_DQK), row(W_DQK), row(W_DV)],
        out_shape=[jax.ShapeDtypeStruct((t, CONV_DIM), F32),
                   jax.ShapeDtypeStruct((t, W_SMALL), F32),
                   jax.ShapeDtypeStruct((t, W_DQK), BF16),
                   jax.ShapeDtypeStruct((t, W_DQK), BF16),
                   jax.ShapeDtypeStruct((t, W_DV), BF16)],
        compiler_params=_cparams("parallel"),
        name="in_proj",
    )(x2, g, w)


def _mla_prep_kernel(small_ref, cos_ref, sin_ref, gq_ref, gkv_ref, wqa_ref, wqb_ref, wk_ref, wv_ref,
                     q_ref, k_ref, v_ref):
    sm = small_ref[0]
    cq = sm[:, :MLA_Q_RANK]
    ckv = sm[:, MLA_Q_RANK:MLA_Q_RANK + MLA_KV_RANK]
    kr = sm[:, MLA_Q_RANK + MLA_KV_RANK:MLA_Q_RANK + MLA_KV_RANK + LANES]
    kr_rot = sm[:, MLA_Q_RANK + MLA_KV_RANK + LANES:]
    cos = cos_ref[0]
    sin = sin_ref[0]
    hq = _rms(cq, gq_ref[...]).astype(BF16)
    hkv = _rms(ckv, gkv_ref[...]).astype(BF16)
    qa = _dot(hq, wqa_ref[...])
    qb = _dot(hq, wqb_ref[...])
    kn = _dot(hkv, wk_ref[...])
    vv = _dot(hkv, wv_ref[...])
    kpe = kr * cos + kr_rot * sin
    lane = lax.broadcasted_iota(jnp.int32, (1, LANES), 1)
    scale = (MLA_NOPE + MLA_ROPE) ** -0.5 * LOG2E
    for h in range(MLA_HEADS):
        sl = slice(h * LANES, (h + 1) * LANES)
        q_ref[0, h] = ((qa[:, sl] * cos + qb[:, sl] * sin) * scale).astype(BF16)
        k_ref[0, h] = (kn[:, sl] + kpe).astype(BF16)
        one_lane = MLA_V if h % 2 == 0 else 0
        v_ref[0, h] = (vv[:, sl] + (lane == one_lane).astype(F32)).astype(BF16)


def _mla_prep(small3, cos_t, sin_t, gq, gkv, wqa, wqb, wk, wv, tm):
    b, s, _ = small3.shape
    full = lambda a: pl.BlockSpec(a.shape, lambda bi, i: (0,) * a.ndim)
    head_out = pl.BlockSpec((1, MLA_HEADS, tm, LANES), lambda bi, i: (bi, 0, i, 0))
    return pl.pallas_call(
        _mla_prep_kernel,
        grid=(b, s // tm),
        in_specs=[pl.BlockSpec((1, tm, W_SMALL), lambda bi, i: (bi, i, 0)),
                  pl.BlockSpec((1, tm, LANES), lambda bi, i: (bi, i, 0)),
                  pl.BlockSpec((1, tm, LANES), lambda bi, i: (bi, i, 0)),
                  full(gq), full(gkv), full(wqa), full(wqb), full(wk), full(wv)],
        out_specs=[head_out, head_out, head_out],
        out_shape=[jax.ShapeDtypeStruct((b, MLA_HEADS, s, LANES), BF16)] * 3,
        compiler_params=_cparams("parallel", "parallel"),
        name="mla_prep",
    )(small3, cos_t, sin_t, gq, gkv, wqa, wqb, wk, wv)


def _add_tile(s, tile):
    rows, tk = s.shape
    return (s.reshape(rows // tile.shape[0], tile.shape[0], tk) + tile[None]).reshape(rows, tk)


def _softmax_update(s, v, state, first, denom_in_acc):
    m_ref, l_ref, acc_ref = state
    rows, tk = s.shape
    chunks = [s[:, c:c + LANES] for c in range(0, tk, LANES)]
    row_max = jnp.max(functools.reduce(jnp.maximum, chunks), axis=1, keepdims=True)
    if first:
        m_new = jnp.broadcast_to(row_max, (rows, LANES))
    else:
        m_old = m_ref[...]
        m_new = jnp.maximum(m_old, row_max)
        alpha = jnp.exp2(m_old - m_new)
    p_chunks = [jnp.exp2(c - m_new) for c in chunks]
    pv = _dot(jnp.concatenate(p_chunks, axis=1).astype(BF16), v)
    if not denom_in_acc:
        row_sum = jnp.sum(functools.reduce(jnp.add, p_chunks), axis=1, keepdims=True)
    if first:
        acc_ref[...] = pv
        if not denom_in_acc:
            l_ref[...] = jnp.broadcast_to(row_sum, (rows, LANES))
    else:
        acc_ref[...] = alpha * acc_ref[...] + pv
        if not denom_in_acc:
            l_ref[...] = alpha * l_ref[...] + row_sum
    m_ref[...] = m_new


def _causal_flash(streams, qi, tq, diag_bias, sub_bias, denom_in_acc, s_buf):
    def blk(ref, j):
        return ref[pl.ds(pl.multiple_of(j * tq, tq), tq), :]

    js = jnp.maximum(qi - 1, 0)
    kill = jnp.where(qi == 0, NEG_BIG, 0.0).astype(F32)
    for si, (q, k_ref, v_ref, state) in enumerate(streams):
        s_diag = _add_tile(_dot_nt(q, blk(k_ref, qi)), diag_bias)
        s_sub = _dot_nt(q, blk(k_ref, js))
        s_buf[0, si] = (s_sub if sub_bias is None else _add_tile(s_sub, sub_bias)) + kill
        _softmax_update(s_diag, blk(v_ref, qi), state, True, denom_in_acc)

    def step(j, src, dst):
        j_prev = jnp.where(j == 0, js, j - 1)
        for si, (q, k_ref, v_ref, state) in enumerate(streams):
            s_next = _dot_nt(q, blk(k_ref, j))
            _softmax_update(s_buf[src, si], blk(v_ref, j_prev), state, False, denom_in_acc)
            s_buf[dst, si] = s_next

    def last(src):
        j_last = jnp.where(js == 0, js, js - 1)
        for si, (q, k_ref, v_ref, state) in enumerate(streams):
            _softmax_update(s_buf[src, si], blk(v_ref, j_last), state, False, denom_in_acc)

    def pair(jj, carry):
        step(2 * jj, 0, 1)
        step(2 * jj + 1, 1, 0)
        return carry

    lax.fori_loop(0, js // 2, pair, 0)

    @pl.when(js % 2 == 1)
    def _():
        step(js - 1, 0, 1)
        last(1)

    @pl.when(js % 2 == 0)
    def _():
        last(0)


def _mla_attn_kernel(q_ref, k_ref, v_ref, o_ref, m_ref, acc_ref, s_buf, *, tq):
    qi = pl.program_id(2)
    row = lax.broadcasted_iota(jnp.int32, (tq, tq), 0)
    col = lax.broadcasted_iota(jnp.int32, (tq, tq), 1)
    mask = jnp.where(row >= col, 0.0, NEG_BIG).astype(F32)
    lane = lax.broadcasted_iota(jnp.int32, (1, LANES), 1)
    streams = [(q_ref[0, hh], k_ref.at[0, hh], v_ref.at[0, hh], (m_ref.at[hh], None, acc_ref.at[hh]))
               for hh in range(2)]
    _causal_flash(streams, qi, tq, mask, None, True, s_buf)
    outs = []
    for hh in range(2):
        acc = acc_ref[hh]
        denom_lane = MLA_V if hh == 0 else 0
        denom = jnp.sum(jnp.where(lane == denom_lane, acc, 0.0), axis=1, keepdims=True)
        outs.append(acc / denom)
    o_ref[0] = jnp.where(lane < MLA_V, outs[0], outs[1]).astype(BF16)


def _mla_attn(q, k, v, tq):
    b, nh, s, _ = q.shape
    return pl.pallas_call(
        functools.partial(_mla_attn_kernel, tq=tq),
        grid=(b, nh // 2, s // tq),
        in_specs=[pl.BlockSpec((1, 2, tq, LANES), lambda bi, hp, i: (bi, hp, i, 0)),
                  pl.BlockSpec((1, 2, s, LANES), lambda bi, hp, i: (bi, hp, 0, 0)),
                  pl.BlockSpec((1, 2, s, LANES), lambda bi, hp, i: (bi, hp, 0, 0))],
        out_specs=pl.BlockSpec((1, tq, LANES), lambda bi, hp, i: (bi, i, hp)),
        out_shape=jax.ShapeDtypeStruct((b, s, nh * MLA_V), BF16),
        scratch_shapes=[pltpu.VMEM((2, tq, LANES), F32), pltpu.VMEM((2, tq, LANES), F32),
                        pltpu.VMEM((2, 2, tq, tq), F32)],
        compiler_params=_cparams("parallel", "parallel", "arbitrary"),
        name="mla_attn",
    )(q, k, v)


def _bucket_upper_bounds():
    n = np.arange(0, 4 * REL_MAX_DIST, dtype=np.int32)
    max_exact = REL_BUCKETS // 2
    nf = np.maximum(n, 1).astype(np.float32)
    large = max_exact + (np.log(nf / np.float32(max_exact)) / np.float32(math.log(REL_MAX_DIST / max_exact))
                         * np.float32(REL_BUCKETS - max_exact)).astype(np.int32)
    large = np.minimum(large, REL_BUCKETS - 1)
    bucket = np.where(n < max_exact, n, large)
    assert np.all(np.diff(bucket) >= 0) and bucket[-1] == REL_BUCKETS - 1
    return [int(np.max(n[bucket == b])) for b in range(REL_BUCKETS - 1)]


def _bias_tile_kernel(tbl_ref, o_ref, *, tq, bounds):
    h = pl.program_id(0)
    kind = pl.program_id(1)
    row = lax.broadcasted_iota(jnp.int32, (tq, tq), 0)
    col = lax.broadcasted_iota(jnp.int32, (tq, tq), 1)
    n = row - col + kind * tq
    far = tbl_ref[(REL_BUCKETS - 1) * DIFF_HEADS + h]
    r = jnp.zeros((tq, tq), F32)
    for bkt in range(REL_BUCKETS - 2, -1, -1):
        r = jnp.where(n <= bounds[bkt], (tbl_ref[bkt * DIFF_HEADS + h] - far) * LOG2E, r)
    o_ref[0, 0] = jnp.where(n >= 0, r, NEG_BIG)


def _bias_tiles(rel_bias, tq):
    bounds = _bucket_upper_bounds()
    assert bounds[-1] < tq, "bias must be constant beyond the sub-diagonal key block"
    return pl.pallas_call(
        functools.partial(_bias_tile_kernel, tq=tq, bounds=bounds),
        grid=(DIFF_HEADS, 2),
        in_specs=[pl.BlockSpec(memory_space=pltpu.SMEM)],
        out_specs=pl.BlockSpec((1, 1, tq, tq), lambda h, kd: (h, kd, 0, 0)),
        out_shape=jax.ShapeDtypeStruct((DIFF_HEADS, 2, tq, tq), F32),
        compiler_params=_cparams("parallel", "parallel"),
        name="rel_bias_tiles",
    )(rel_bias.reshape(-1))


def _diff_attn_kernel(q_ref, k_ref, v_ref, bias_ref, lq1_ref, lk1_ref, lq2_ref, lk2_ref, gsub_ref,
                      o_ref, m_ref, l_ref, acc_ref, s_buf, *, tq, lam_init):
    qi = pl.program_id(2)
    lam = (jnp.exp(jnp.sum(lq1_ref[...] * lk1_ref[...], axis=1, keepdims=True))
           - jnp.exp(jnp.sum(lq2_ref[...] * lk2_ref[...], axis=1, keepdims=True)) + lam_init)
    q = q_ref[0]
    lane = lax.broadcasted_iota(jnp.int32, (1, LANES), 1)
    zero = jnp.zeros_like(q)
    q_maps = (jnp.where(lane < DIFF_HEAD, q, zero), jnp.where(lane >= DIFF_HEAD, q, zero))
    streams = [(q_maps[mp], k_ref.at[0], v_ref.at[0], (m_ref.at[mp], l_ref.at[mp], acc_ref.at[mp]))
               for mp in range(2)]
    _causal_flash(streams, qi, tq, bias_ref[0, 0], bias_ref[0, 1], False, s_buf)
    o = acc_ref[0] / l_ref[0] - lam * (acc_ref[1] / l_ref[1])
    o_ref[0] = (_rms(o, gsub_ref[...]) * (1.0 - lam_init)).astype(BF16)


def _diff_attn(dq, dk, dv, bias_tiles, lq1, lk1, lq2, lk2, gsub, lam_init, tq):
    b, s, _ = dq.shape
    vec = lambda a: pl.BlockSpec(a.shape, lambda bi, h, i: (0, 0))
    return pl.pallas_call(
        functools.partial(_diff_attn_kernel, tq=tq, lam_init=lam_init),
        grid=(b, DIFF_HEADS, s // tq),
        in_specs=[pl.BlockSpec((1, tq, LANES), lambda bi, h, i: (bi, i, h)),
                  pl.BlockSpec((1, s, LANES), lambda bi, h, i: (bi, 0, h)),
                  pl.BlockSpec((1, s, LANES), lambda bi, h, i: (bi, 0, h)),
                  pl.BlockSpec((1, 2, tq, tq), lambda bi, h, i: (h, 0, 0, 0)),
                  vec(lq1), vec(lk1), vec(lq2), vec(lk2), vec(gsub)],
        out_specs=pl.BlockSpec((1, tq, LANES), lambda bi, h, i: (bi, i, h)),
        out_shape=jax.ShapeDtypeStruct((b, s, W_DV), BF16),
        scratch_shapes=[pltpu.VMEM((2, tq, LANES), F32), pltpu.VMEM((2, tq, LANES), F32),
                        pltpu.VMEM((2, tq, LANES), F32), pltpu.VMEM((2, 2, tq, tq), F32)],
        compiler_params=_cparams("parallel", "parallel", "arbitrary"),
        name="diff_attn",
    )(dq, dk, dv, bias_tiles, lq1, lk1, lq2, lk2, gsub)


def _conv_kernel(a_ref, halo_ref, w_ref, b_ref, lng_ref, lnb_ref, o_ref, buf_ref, *, tm):
    i = pl.program_id(1)
    halo = halo_ref[0]
    buf_ref[0:CONV_HALO, :] = jnp.where(i == 0, jnp.zeros_like(halo), halo)
    buf_ref[CONV_HALO:, :] = a_ref[0]
    base = CONV_HALO - (CONV_WIDTH - 1)
    y = jnp.zeros((tm, CONV_DIM), F32) + b_ref[...]
    for j in range(CONV_WIDTH):
        y = y + w_ref[j:j + 1, :] * buf_ref[base + j:base + j + tm, :]
    mu = jnp.mean(y, axis=-1, keepdims=True)
    var = jnp.mean(jnp.square(y - mu), axis=-1, keepdims=True)
    z = (y - mu) * lax.rsqrt(var + EPS) * lng_ref[...] + lnb_ref[...]
    o_ref[0] = (z * jax.nn.sigmoid(z)).astype(BF16)


def _conv_branch(a3, w, bias, lng, lnb, tm):
    b, s, _ = a3.shape
    per = tm // CONV_HALO
    full = lambda a: pl.BlockSpec(a.shape, lambda bi, i: (0, 0))
    return pl.pallas_call(
        functools.partial(_conv_kernel, tm=tm),
        grid=(b, s // tm),
        in_specs=[pl.BlockSpec((1, tm, CONV_DIM), lambda bi, i: (bi, i, 0)),
                  pl.BlockSpec((1, CONV_HALO, CONV_DIM), lambda bi, i: (bi, jnp.maximum(i * per - 1, 0), 0)),
                  full(w), full(bias), full(lng), full(lnb)],
        out_specs=pl.BlockSpec((1, tm, CONV_DIM), lambda bi, i: (bi, i, 0)),
        out_shape=jax.ShapeDtypeStruct((b, s, CONV_DIM), BF16),
        scratch_shapes=[pltpu.VMEM((tm + CONV_HALO, CONV_DIM), F32)],
        compiler_params=_cparams("parallel", "parallel"),
        name="conv_branch",
    )(a3, a3, w, bias, lng, lnb)


def _merge_kernel(x_ref, g_ref, wg_ref, bg_ref, ya_ref, yb_ref, yc_ref, wa_ref, wb_ref, wc_ref, wo_ref, o_ref):
    x = x_ref[...]
    h = _rms(x, g_ref[...]).astype(BF16)
    merged = None
    for br, (y_ref, w_ref) in enumerate(((ya_ref, wa_ref), (yb_ref, wb_ref), (yc_ref, wc_ref))):
        sl = slice(br * D_MODEL, (br + 1) * D_MODEL)
        gate = jax.nn.sigmoid(_dot(h, wg_ref[:, sl]) + bg_ref[:, sl])
        term = gate * _dot(y_ref[...], w_ref[...])
        merged = term if merged is None else merged + term
    o_ref[...] = x + _dot(merged.astype(BF16), wo_ref[...])


def _merge(x2, g, wg, bg, ya, yb, yc, wa, wb, wc, wo, tm):
    t = x2.shape[0]
    row = lambda n: pl.BlockSpec((tm, n), lambda i: (i, 0))
    full = lambda a: pl.BlockSpec(a.shape, lambda i: (0, 0))
    return pl.pallas_call(
        _merge_kernel,
        grid=(t // tm,),
        in_specs=[row(D_MODEL), full(g), full(wg), full(bg), row(CONV_DIM), row(MLA_HEADS * MLA_V), row(W_DV),
                  full(wa), full(wb), full(wc), full(wo)],
        out_specs=row(D_MODEL),
        out_shape=jax.ShapeDtypeStruct((t, D_MODEL), F32),
        compiler_params=_cparams("parallel"),
        name="merge",
    )(x2, g, wg, bg, ya, yb, yc, wa, wb, wc, wo)


def _ffn_kernel(x_ref, g_ref, wv_ref, wg_ref, cw_ref, cb_ref, wd_ref, gf_ref, o_ref,
                h_ref, acc_ref, buf_ref, carry_ref, *, tm, tiles_per_seq, final_norm):
    i = pl.program_id(0)
    x = x_ref[...]
    h_ref[...] = _rms(x, g_ref[...]).astype(BF16)
    acc_ref[...] = jnp.zeros_like(acc_ref)
    pad = SUBLANES

    @pl.when((i % tiles_per_seq) == 0)
    def _():
        carry_ref[...] = jnp.zeros_like(carry_ref)

    def up(c, slot):
        h = h_ref[...]
        buf_ref[slot, 0, pad:, :] = _dot(h, wv_ref[c])
        buf_ref[slot, 1, pad:, :] = _dot(h, wg_ref[c])

    def conv3(c, slot, half):
        buf = buf_ref.at[slot, half]
        buf[0:pad, :] = carry_ref[c, half]
        carry_ref[c, half] = buf[tm:tm + pad, :]
        w = cw_ref[c, half]
        return (w[0:1, :] * buf[pad - 2:pad - 2 + tm, :] + w[1:2, :] * buf[pad - 1:pad - 1 + tm, :]
                + w[2:3, :] * buf[pad:pad + tm, :] + cb_ref[c, half])

    def down(c, slot):
        uv = conv3(c, slot, 0)
        ug = conv3(c, slot, 1)
        act = (uv * jax.nn.sigmoid(uv) * ug).astype(BF16)
        acc_ref[...] += _dot(act, wd_ref[c])

    def pair(cc, carry):
        c = 2 * cc
        up(c + 1, 1)
        down(c, 0)
        up(c + 2, 0)
        down(c + 1, 1)
        return carry

    up(0, 0)
    n_pairs = (N_FF_CHUNK - 1) // 2
    lax.fori_loop(0, n_pairs, pair, 0)
    if (N_FF_CHUNK - 1) % 2 == 1:
        up(N_FF_CHUNK - 1, 1)
        down(N_FF_CHUNK - 2, 0)
        down(N_FF_CHUNK - 1, 1)
    else:
        down(N_FF_CHUNK - 1, 0)
    y = x + acc_ref[...]
    if final_norm:
        y = _rms(y, gf_ref[...])
    o_ref[...] = y


def _ffn(x2, g, wv, wg, cw, cb, wd, gf, tm, tiles_per_seq, final_norm):
    t = x2.shape[0]
    row = pl.BlockSpec((tm, D_MODEL), lambda i: (i, 0))
    full = lambda a: pl.BlockSpec(a.shape, lambda i: (0,) * a.ndim, pipeline_mode=pl.Buffered(1))
    return pl.pallas_call(
        functools.partial(_ffn_kernel, tm=tm, tiles_per_seq=tiles_per_seq, final_norm=final_norm),
        grid=(t // tm,),
        in_specs=[row, full(g), full(wv), full(wg), full(cw), full(cb), full(wd), full(gf)],
        out_specs=row,
        out_shape=jax.ShapeDtypeStruct((t, D_MODEL), F32),
        scratch_shapes=[pltpu.VMEM((tm, D_MODEL), BF16), pltpu.VMEM((tm, D_MODEL), F32),
                        pltpu.VMEM((2, 2, tm + SUBLANES, FF_CHUNK), F32),
                        pltpu.VMEM((N_FF_CHUNK, 2, SUBLANES, FF_CHUNK), F32)],
        compiler_params=_cparams("arbitrary"),
        name="ffn",
    )(x2, g, wv, wg, cw, cb, wd, gf)


def _rot_cols(w):
    half = w.shape[-1] // 2
    return jnp.concatenate([-w[..., half:], w[..., :half]], axis=-1)


def _prep_in_proj_weight(w_in_l):
    cuts = np.cumsum([W_GLU, MLA_Q_RANK, MLA_KV_RANK, MLA_ROPE, W_DQK, W_DQK, W_DV])
    glu, cq, ckv, kr, dq, dk, dv, gates = jnp.split(w_in_l, cuts, axis=1)
    z = lambda n: jnp.zeros((D_MODEL, n), w_in_l.dtype)
    tail = LANES - MLA_NOPE - MLA_ROPE
    w = jnp.concatenate([glu, cq, ckv, z(MLA_NOPE), kr, z(tail), z(MLA_NOPE), _rot_cols(kr), z(tail), dq, dk, dv],
                        axis=1)
    return w.astype(BF16), gates.astype(BF16)


def _prep_mla_weights(w_uq_l, w_ukv_l):
    dqk = MLA_NOPE + MLA_ROPE
    wq = w_uq_l.reshape(MLA_Q_RANK, MLA_HEADS, dqk)
    pad = jnp.zeros((MLA_Q_RANK, MLA_HEADS, LANES - dqk), wq.dtype)
    wqa = jnp.concatenate([wq, pad], axis=-1)
    wqb = jnp.concatenate([jnp.zeros_like(wq[..., :MLA_NOPE]), _rot_cols(wq[..., MLA_NOPE:]), pad], axis=-1)
    wkv = w_ukv_l.reshape(MLA_KV_RANK, MLA_HEADS, MLA_NOPE + MLA_V)
    zk = jnp.zeros((MLA_KV_RANK, MLA_HEADS, LANES - MLA_NOPE), wkv.dtype)
    wk = jnp.concatenate([wkv[..., :MLA_NOPE], zk], axis=-1)
    zv = jnp.zeros((MLA_KV_RANK, MLA_HEADS, LANES - MLA_V), wkv.dtype)
    v_even = jnp.concatenate([wkv[..., MLA_NOPE:], zv], axis=-1)
    v_odd = jnp.concatenate([zv, wkv[..., MLA_NOPE:]], axis=-1)
    odd = (jnp.arange(MLA_HEADS) % 2 == 1)[None, :, None]
    wv = jnp.where(odd, v_odd, v_even)
    flat = lambda a: a.reshape(a.shape[0], MLA_HEADS * LANES).astype(BF16)
    return flat(wqa), flat(wqb), flat(wk), flat(wv)


def _rope_tables(positions):
    half = MLA_ROPE // 2
    freqs = ROPE_THETA ** (-jnp.arange(half, dtype=F32) / half)
    ang = positions.astype(F32)[..., None] * freqs
    cos, sin = jnp.cos(ang), jnp.sin(ang)
    b, s = positions.shape
    ones = jnp.ones((b, s, MLA_NOPE), F32)
    zeros_n = jnp.zeros((b, s, MLA_NOPE), F32)
    zeros_t = jnp.zeros((b, s, LANES - MLA_NOPE - MLA_ROPE), F32)
    cos_t = jnp.concatenate([ones, cos, cos, zeros_t], axis=-1)
    sin_t = jnp.concatenate([zeros_n, sin, sin, zeros_t], axis=-1)
    return cos_t, sin_t


def _chunk_cols(w, n):
    return jnp.moveaxis(w.reshape(w.shape[0], n, FF_CHUNK), 1, 0)


def kernel(x, positions, rel_bias, norm_mix, w_in, gate_bias, conv_w, conv_b, conv_ln_g, conv_ln_b, w_conv_out,
           mla_q_norm, w_uq, mla_kv_norm, w_ukv, w_mla_out, diff_lam_q1, diff_lam_k1, diff_lam_q2, diff_lam_k2,
           diff_sub_norm, w_diff_out, w_out, norm_ffn, w_up, ffn_conv_w, ffn_conv_b, w_down, norm_final):
    b, s, d = x.shape
    depth = w_in.shape[0]
    assert d == D_MODEL
    tm = min(512, s)
    tm_ffn = min(1024, s)
    tq = min(512, s)
    assert s % tm == 0 and s % tm_ffn == 0 and s % tq == 0 and tm % CONV_HALO == 0
    t = b * s
    row2 = lambda a: a.reshape(1, -1)

    cos_t, sin_t = _rope_tables(positions)
    bias_tiles = _bias_tiles(rel_bias, tq)
    x2 = x.reshape(t, d)

    for l in range(depth):
        w_in_p, w_gate = _prep_in_proj_weight(w_in[l])
        a, small, dq, dk, dv = _in_proj(x2, row2(norm_mix[l]), w_in_p, tm)

        y_a = _conv_branch(a.reshape(b, s, CONV_DIM), conv_w[l], row2(conv_b[l]), row2(conv_ln_g[l]),
                           row2(conv_ln_b[l]), tm)

        wqa, wqb, wk, wv = _prep_mla_weights(w_uq[l], w_ukv[l])
        q, k, v = _mla_prep(small.reshape(b, s, W_SMALL), cos_t, sin_t, row2(mla_q_norm[l]), row2(mla_kv_norm[l]),
                            wqa, wqb, wk, wv, tm)
        y_b = _mla_attn(q, k, v, tq)

        lam_init = 0.8 - 0.6 * math.exp(-0.3 * l)
        y_c = _diff_attn(dq.reshape(b, s, W_DQK), dk.reshape(b, s, W_DQK), dv.reshape(b, s, W_DV), bias_tiles,
                         row2(diff_lam_q1[l]), row2(diff_lam_k1[l]), row2(diff_lam_q2[l]), row2(diff_lam_k2[l]),
                         row2(diff_sub_norm[l]), lam_init, tq)

        x2 = _merge(x2, row2(norm_mix[l]), w_gate, row2(gate_bias[l]),
                    y_a.reshape(t, CONV_DIM), y_b.reshape(t, MLA_HEADS * MLA_V), y_c.reshape(t, W_DV),
                    w_conv_out[l].astype(BF16), w_mla_out[l].astype(BF16), w_diff_out[l].astype(BF16),
                    w_out[l].astype(BF16), tm)

        w_up_l = w_up[l].astype(BF16)
        cw = ffn_conv_w[l].reshape(FFN_CONV, 2, N_FF_CHUNK, FF_CHUNK).transpose(2, 1, 0, 3)
        cb = ffn_conv_b[l].reshape(2, N_FF_CHUNK, 1, FF_CHUNK).transpose(1, 0, 2, 3)
        x2 = _ffn(x2, row2(norm_ffn[l]), _chunk_cols(w_up_l[:, :D_FF], N_FF_CHUNK),
                  _chunk_cols(w_up_l[:, D_FF:], N_FF_CHUNK), cw, cb,
                  w_down[l].astype(BF16).reshape(N_FF_CHUNK, FF_CHUNK, D_MODEL), row2(norm_final),
                  tm_ffn, s // tm_ffn, l == depth - 1)
    return x2.reshape(b, s, d)
```

```python
import functools
import math

import numpy as np
import jax
import jax.numpy as jnp
from jax import lax
from jax.experimental import pallas as pl
from jax.experimental.pallas import tpu as pltpu

F32 = jnp.float32
BF16 = jnp.bfloat16

D_MODEL = 1024
CONV_DIM = 512
CONV_WIDTH = 31
MLA_HEADS = 8
MLA_NOPE = 64
MLA_ROPE = 32
MLA_V = 64
MLA_Q_RANK = 256
MLA_KV_RANK = 128
ROPE_THETA = 10000.0
DIFF_HEADS = 4
DIFF_HEAD = 64
DIFF_V = 2 * DIFF_HEAD
REL_BUCKETS = 32
REL_MAX_DIST = 128
D_FF = 2816
FFN_CONV = 3
N_BRANCH = 3
EPS = 1e-6

LANES = 128
SUBLANES = 8
VMEM_LIMIT = 56 * 1024 * 1024
NEG_BIG = -1e30
LOG2E = math.log2(math.e)

W_GLU = 2 * CONV_DIM
W_DQK = DIFF_HEADS * 2 * DIFF_HEAD
W_DV = DIFF_HEADS * DIFF_V
W_GATE = N_BRANCH * D_MODEL
W_SMALL = MLA_Q_RANK + MLA_KV_RANK + 2 * LANES
FF_CHUNK = 256
N_FF_CHUNK = D_FF // FF_CHUNK
CONV_HALO = 32


def _cparams(*sem):
    return pltpu.CompilerParams(dimension_semantics=sem, vmem_limit_bytes=VMEM_LIMIT)


def _rms(x, g):
    return x * lax.rsqrt(jnp.mean(x * x, axis=-1, keepdims=True) + EPS) * g


def _dot(a, b):
    return jnp.dot(a, b, preferred_element_type=F32)


def _dot_nt(a, b):
    return lax.dot_general(a, b, (((1,), (1,)), ((), ())), preferred_element_type=F32)


def _in_proj_kernel(x_ref, g_ref, w_ref, a_ref, small_ref, dq_ref, dk_ref, dv_ref):
    h = _rms(x_ref[...], g_ref[...]).astype(BF16)
    c0 = 0
    u = _dot(h, w_ref[:, c0:c0 + W_GLU])
    a_ref[...] = u[:, :CONV_DIM] * jax.nn.sigmoid(u[:, CONV_DIM:])
    c0 += W_GLU
    small_ref[...] = _dot(h, w_ref[:, c0:c0 + W_SMALL])
    c0 += W_SMALL
    dq_ref[...] = (_dot(h, w_ref[:, c0:c0 + W_DQK]) * (DIFF_HEAD ** -0.5 * LOG2E)).astype(BF16)
    c0 += W_DQK
    dk_ref[...] = _dot(h, w_ref[:, c0:c0 + W_DQK]).astype(BF16)
    c0 += W_DQK
    dv_ref[...] = _dot(h, w_ref[:, c0:c0 + W_DV]).astype(BF16)


def _in_proj(x2, g, w, tm):
    t = x2.shape[0]
    n_w = w.shape[1]
    row = lambda n: pl.BlockSpec((tm, n), lambda i: (i, 0))
    full = lambda a, b: pl.BlockSpec((a, b), lambda i: (0, 0))
    return pl.pallas_call(
        _in_proj_kernel,
        grid=(t // tm,),
        in_specs=[row(D_MODEL), full(1, D_MODEL), full(D_MODEL, n_w)],
        out_specs=[row(CONV_DIM), row(W_SMALL), row(W_DQK), row(W_DQK), row(W_DV)],
        out_shape=[jax.ShapeDtypeStruct((t, CONV_DIM), F32),
                   jax.ShapeDtypeStruct((t, W_SMALL), F32),
                   jax.ShapeDtypeStruct((t, W_DQK), BF16),
                   jax.ShapeDtypeStruct((t, W_DQK), BF16),
                   jax.ShapeDtypeStruct((t, W_DV), BF16)],
        compiler_params=_cparams("parallel"),
        name="in_proj",
    )(x2, g, w)


def _mla_prep_kernel(small_ref, cos_ref, sin_ref, gq_ref, gkv_ref, wqa_ref, wqb_ref, wk_ref, wv_ref,
                     q_ref, k_ref, v_ref):
    sm = small_ref[0]
    cq = sm[:, :MLA_Q_RANK]
    ckv = sm[:, MLA_Q_RANK:MLA_Q_RANK + MLA_KV_RANK]
    kr = sm[:, MLA_Q_RANK + MLA_KV_RANK:MLA_Q_RANK + MLA_KV_RANK + LANES]
    kr_rot = sm[:, MLA_Q_RANK + MLA_KV_RANK + LANES:]
    cos = cos_ref[0]
    sin = sin_ref[0]
    hq = _rms(cq, gq_ref[...]).astype(BF16)
    hkv = _rms(ckv, gkv_ref[...]).astype(BF16)
    qa = _dot(hq, wqa_ref[...])
    qb = _dot(hq, wqb_ref[...])
    kn = _dot(hkv, wk_ref[...])
    vv = _dot(hkv, wv_ref[...])
    kpe = kr * cos + kr_rot * sin
    lane = lax.broadcasted_iota(jnp.int32, (1, LANES), 1)
    scale = (MLA_NOPE + MLA_ROPE) ** -0.5 * LOG2E
    for h in range(MLA_HEADS):
        sl = slice(h * LANES, (h + 1) * LANES)
        q_ref[0, h] = ((qa[:, sl] * cos + qb[:, sl] * sin) * scale).astype(BF16)
        k_ref[0, h] = (kn[:, sl] + kpe).astype(BF16)
        one_lane = MLA_V if h % 2 == 0 else 0
        v_ref[0, h] = (vv[:, sl] + (lane == one_lane).astype(F32)).astype(BF16)


def _mla_prep(small3, cos_t, sin_t, gq, gkv, wqa, wqb, wk, wv, tm):
    b, s, _ = small3.shape
    full = lambda a: pl.BlockSpec(a.shape, lambda bi, i: (0,) * a.ndim)
    head_out = pl.BlockSpec((1, MLA_HEADS, tm, LANES), lambda bi, i: (bi, 0, i, 0))
    return pl.pallas_call(
        _mla_prep_kernel,
        grid=(b, s // tm),
        in_specs=[pl.BlockSpec((1, tm, W_SMALL), lambda bi, i: (bi, i, 0)),
                  pl.BlockSpec((1, tm, LANES), lambda bi, i: (bi, i, 0)),
                  pl.BlockSpec((1, tm, LANES), lambda bi, i: (bi, i, 0)),
                  full(gq), full(gkv), full(wqa), full(wqb), full(wk), full(wv)],
        out_specs=[head_out, head_out, head_out],
        out_shape=[jax.ShapeDtypeStruct((b, MLA_HEADS, s, LANES), BF16)] * 3,
        compiler_params=_cparams("parallel", "parallel"),
        name="mla_prep",
    )(small3, cos_t, sin_t, gq, gkv, wqa, wqb, wk, wv)


def _score_tile(q, k, bias, s_ref, rmax_ref):
    s = _dot_nt(q, k) + bias
    s_ref[...] = s
    chunks = [s[:, c:c + LANES] for c in range(0, s.shape[1], LANES)]
    row_max = jnp.max(functools.reduce(jnp.maximum, chunks), axis=1, keepdims=True)
    rmax_ref[...] = jnp.broadcast_to(row_max, rmax_ref.shape)


def _softmax_update(s_ref, rmax_ref, v, m_ref, acc_ref):
    m_old = m_ref[...]
    m_new = jnp.maximum(m_old, rmax_ref[...])
    alpha = jnp.exp2(m_old - m_new)
    p = jnp.concatenate([jnp.exp2(s_ref[:, c:c + LANES] - m_new) for c in range(0, s_ref.shape[1], LANES)], axis=1)
    pv = _dot(p.astype(BF16), v)
    acc_ref[...] = jnp.concatenate([alpha] * (acc_ref.shape[1] // LANES), axis=1) * acc_ref[...] + pv
    m_ref[...] = m_new


def _causal_tiles(nq):
    return [(qi, kj) for qi in range(nq) for kj in range(qi + 1)]


def _tile_tables(nq):
    tiles = np.asarray(_causal_tiles(nq), dtype=np.int32)
    return jnp.asarray(tiles[:, 0]), jnp.asarray(tiles[:, 1])


def _causal_flash(streams, qtab, ktab, n_tiles, tq, bias_ref, s_buf, rmax_buf):
    n_kinds = bias_ref.shape[0]

    def blk(ref, j):
        return ref[pl.ds(pl.multiple_of(j * tq, tq), tq), :]

    def scores(t, dst):
        qi, kj = qtab[t], ktab[t]
        bias = bias_ref[jnp.minimum(qi - kj, n_kinds - 1)]
        for si, (q_ref, k_ref, _, _, _) in enumerate(streams):
            _score_tile(blk(q_ref, qi), blk(k_ref, kj), bias, s_buf.at[dst, si], rmax_buf.at[dst, si])

    def update(t, src):
        qi, kj = qtab[t], ktab[t]
        for si, (_, _, v_ref, m_ref, acc_ref) in enumerate(streams):
            _softmax_update(s_buf.at[src, si], rmax_buf.at[src, si], blk(v_ref, kj), m_ref.at[qi], acc_ref.at[qi])

    def pair(jj, carry):
        t = 2 * jj
        scores(t + 1, 1)
        update(t, 0)
        scores(t + 2, 0)
        update(t + 1, 1)
        return carry

    for _, _, _, m_ref, acc_ref in streams:
        m_ref[...] = jnp.full(m_ref.shape, NEG_BIG, F32)
        acc_ref[...] = jnp.zeros(acc_ref.shape, F32)
    scores(0, 0)
    lax.fori_loop(0, (n_tiles - 1) // 2, pair, 0)
    if (n_tiles - 1) % 2 == 1:
        scores(n_tiles - 1, 1)
        update(n_tiles - 2, 0)
        update(n_tiles - 1, 1)
    else:
        update(n_tiles - 1, 0)


def _mla_attn_kernel(qtab, ktab, q_ref, k_ref, v_ref, o_ref, m_ref, acc_ref, s_buf, rmax_buf, mask_ref,
                     *, tq, nq):
    row = lax.broadcasted_iota(jnp.int32, (tq, tq), 0)
    col = lax.broadcasted_iota(jnp.int32, (tq, tq), 1)
    mask_ref[0] = jnp.where(row >= col, 0.0, NEG_BIG).astype(F32)
    mask_ref[1] = jnp.zeros((tq, tq), F32)
    streams = [(q_ref.at[0, hh], k_ref.at[0, hh], v_ref.at[0, hh], m_ref.at[hh], acc_ref.at[hh])
               for hh in range(2)]
    _causal_flash(streams, qtab, ktab, nq * (nq + 1) // 2, tq, mask_ref, s_buf, rmax_buf)
    lane = lax.broadcasted_iota(jnp.int32, (1, LANES), 1)
    for qi in range(nq):
        outs = []
        for hh in range(2):
            acc = acc_ref[hh, qi]
            denom_lane = MLA_V if hh == 0 else 0
            denom = jnp.sum(jnp.where(lane == denom_lane, acc, 0.0), axis=1, keepdims=True)
            outs.append(acc / denom)
        o_ref[0, qi * tq:(qi + 1) * tq, :] = jnp.where(lane < MLA_V, outs[0], outs[1]).astype(BF16)


def _mla_attn(q, k, v, tq):
    b, nh, s, _ = q.shape
    nq = s // tq
    qtab, ktab = _tile_tables(nq)
    smem = pl.BlockSpec(memory_space=pltpu.SMEM)
    seq = pl.BlockSpec((1, 2, s, LANES), lambda bi, hp: (bi, hp, 0, 0))
    return pl.pallas_call(
        functools.partial(_mla_attn_kernel, tq=tq, nq=nq),
        grid=(b, nh // 2),
        in_specs=[smem, smem, seq, seq, seq],
        out_specs=pl.BlockSpec((1, s, LANES), lambda bi, hp: (bi, 0, hp)),
        out_shape=jax.ShapeDtypeStruct((b, s, nh * MLA_V), BF16),
        scratch_shapes=[pltpu.VMEM((2, nq, tq, LANES), F32), pltpu.VMEM((2, nq, tq, LANES), F32),
                        pltpu.VMEM((2, 2, tq, tq), F32), pltpu.VMEM((2, 2, tq, LANES), F32),
                        pltpu.VMEM((2, tq, tq), F32)],
        compiler_params=_cparams("parallel", "parallel"),
        name="mla_attn",
    )(qtab, ktab, q, k, v)


def _bucket_upper_bounds():
    n = np.arange(0, 4 * REL_MAX_DIST, dtype=np.int32)
    max_exact = REL_BUCKETS // 2
    nf = np.maximum(n, 1).astype(np.float32)
    large = max_exact + (np.log(nf / np.float32(max_exact)) / np.float32(math.log(REL_MAX_DIST / max_exact))
                         * np.float32(REL_BUCKETS - max_exact)).astype(np.int32)
    large = np.minimum(large, REL_BUCKETS - 1)
    bucket = np.where(n < max_exact, n, large)
    assert np.all(np.diff(bucket) >= 0) and bucket[-1] == REL_BUCKETS - 1
    return [int(np.max(n[bucket == b])) for b in range(REL_BUCKETS - 1)]


def _bias_tile_kernel(tbl_ref, o_ref, *, tq, bounds):
    h = pl.program_id(0)
    kind = pl.program_id(1)
    row = lax.broadcasted_iota(jnp.int32, (tq, tq), 0)
    col = lax.broadcasted_iota(jnp.int32, (tq, tq), 1)
    n = row - col + kind * tq
    far = tbl_ref[(REL_BUCKETS - 1) * DIFF_HEADS + h]
    r = jnp.zeros((tq, tq), F32)
    for bkt in range(REL_BUCKETS - 2, -1, -1):
        r = jnp.where(n <= bounds[bkt], (tbl_ref[bkt * DIFF_HEADS + h] - far) * LOG2E, r)
    o_ref[0, 0] = jnp.where(n >= 0, r, NEG_BIG)


def _bias_tiles(rel_bias, tq):
    bounds = _bucket_upper_bounds()
    assert bounds[-1] < tq, "bias must be constant from the second key block before the diagonal on"
    return pl.pallas_call(
        functools.partial(_bias_tile_kernel, tq=tq, bounds=bounds),
        grid=(DIFF_HEADS, 3),
        in_specs=[pl.BlockSpec(memory_space=pltpu.SMEM)],
        out_specs=pl.BlockSpec((1, 1, tq, tq), lambda h, kd: (h, kd, 0, 0)),
        out_shape=jax.ShapeDtypeStruct((DIFF_HEADS, 3, tq, tq), F32),
        compiler_params=_cparams("parallel", "parallel"),
        name="rel_bias_tiles",
    )(rel_bias.reshape(-1))


def _diff_attn_kernel(qtab, ktab, q_ref, k_ref, v_ref, bias_ref, lq1_ref, lk1_ref, lq2_ref, lk2_ref, gsub_ref,
                      o_ref, m_ref, acc_ref, s_buf, rmax_buf, qm_ref, vx_ref, *, tq, nq, lam_init):
    lam = (jnp.exp(jnp.sum(lq1_ref[...] * lk1_ref[...], axis=1, keepdims=True))
           - jnp.exp(jnp.sum(lq2_ref[...] * lk2_ref[...], axis=1, keepdims=True)) + lam_init)
    q = q_ref[0]
    lane = lax.broadcasted_iota(jnp.int32, (1, LANES), 1)
    zero = jnp.zeros_like(q)
    qm_ref[0] = jnp.where(lane < DIFF_HEAD, q, zero)
    qm_ref[1] = jnp.where(lane >= DIFF_HEAD, q, zero)
    vx_ref[:, :DIFF_V] = v_ref[0]
    vx_ref[:, DIFF_V:] = jnp.broadcast_to((lane == 0).astype(BF16), (vx_ref.shape[0], LANES))
    streams = [(qm_ref.at[mp], k_ref.at[0], vx_ref, m_ref.at[mp], acc_ref.at[mp]) for mp in range(2)]
    _causal_flash(streams, qtab, ktab, nq * (nq + 1) // 2, tq, bias_ref.at[0], s_buf, rmax_buf)
    for qi in range(nq):
        outs = []
        for mp in range(2):
            acc = acc_ref[mp, qi]
            outs.append(acc[:, :DIFF_V] / jnp.sum(acc[:, DIFF_V:], axis=1, keepdims=True))
        o = outs[0] - lam * outs[1]
        o_ref[0, qi * tq:(qi + 1) * tq, :] = (_rms(o, gsub_ref[...]) * (1.0 - lam_init)).astype(BF16)


def _diff_attn(dq, dk, dv, bias_tiles, lq1, lk1, lq2, lk2, gsub, lam_init, tq):
    b, s, _ = dq.shape
    nq = s // tq
    qtab, ktab = _tile_tables(nq)
    smem = pl.BlockSpec(memory_space=pltpu.SMEM)
    vec = lambda a: pl.BlockSpec(a.shape, lambda bi, h: (0, 0))
    seq = pl.BlockSpec((1, s, LANES), lambda bi, h: (bi, 0, h))
    n_kinds = bias_tiles.shape[1]
    return pl.pallas_call(
        functools.partial(_diff_attn_kernel, tq=tq, nq=nq, lam_init=lam_init),
        grid=(b, DIFF_HEADS),
        in_specs=[smem, smem, seq, seq, seq,
                  pl.BlockSpec((1, n_kinds, tq, tq), lambda bi, h: (h, 0, 0, 0)),
                  vec(lq1), vec(lk1), vec(lq2), vec(lk2), vec(gsub)],
        out_specs=seq,
        out_shape=jax.ShapeDtypeStruct((b, s, W_DV), BF16),
        scratch_shapes=[pltpu.VMEM((2, nq, tq, LANES), F32), pltpu.VMEM((2, nq, tq, 2 * LANES), F32),
                        pltpu.VMEM((2, 2, tq, tq), F32), pltpu.VMEM((2, 2, tq, LANES), F32),
                        pltpu.VMEM((2, s, LANES), BF16), pltpu.VMEM((s, 2 * LANES), BF16)],
        compiler_params=_cparams("parallel", "parallel"),
        name="diff_attn",
    )(qtab, ktab, dq, dk, dv, bias_tiles, lq1, lk1, lq2, lk2, gsub)


def _conv_ln_silu(a_ref, halo_ref, w_ref, b_ref, lng_ref, lnb_ref, buf_ref, shift_ref, seq_start, tm):
    halo = halo_ref[0]
    buf_ref[0:CONV_HALO, :] = jnp.where(seq_start, jnp.zeros_like(halo), halo)
    buf_ref[CONV_HALO:, :] = a_ref[0]
    n_rows = tm + CONV_HALO - SUBLANES
    for r in range(1, SUBLANES):
        shift_ref[r - 1] = buf_ref[r:r + n_rows, :]
    base = CONV_HALO - (CONV_WIDTH - 1)
    y = jnp.zeros((tm, CONV_DIM), F32) + b_ref[...]
    for j in range(CONV_WIDTH):
        grp, r = divmod(base + j, SUBLANES)
        lo = grp * SUBLANES
        tap = buf_ref[lo:lo + tm, :] if r == 0 else shift_ref[r - 1, lo:lo + tm, :]
        y = y + w_ref[j:j + 1, :] * tap
    mu = jnp.mean(y, axis=-1, keepdims=True)
    var = jnp.mean(jnp.square(y - mu), axis=-1, keepdims=True)
    z = (y - mu) * lax.rsqrt(var + EPS) * lng_ref[...] + lnb_ref[...]
    return (z * jax.nn.sigmoid(z)).astype(BF16)


def _merge_kernel(x_ref, g_ref, wg_ref, bg_ref, a_ref, halo_ref, cw_ref, cb_ref, lng_ref, lnb_ref, yb_ref, yc_ref,
                  wa_ref, wb_ref, wc_ref, wo_ref, o_ref, buf_ref, shift_ref, *, tm):
    x = x_ref[0]
    h = _rms(x, g_ref[...]).astype(BF16)
    ya = _conv_ln_silu(a_ref, halo_ref, cw_ref, cb_ref, lng_ref, lnb_ref, buf_ref, shift_ref,
                       pl.program_id(1) == 0, tm)
    merged = None
    for br, (y, w_ref) in enumerate(((ya, wa_ref), (yb_ref[0], wb_ref), (yc_ref[0], wc_ref))):
        sl = slice(br * D_MODEL, (br + 1) * D_MODEL)
        gate = jax.nn.sigmoid(_dot(h, wg_ref[:, sl]) + bg_ref[:, sl])
        term = gate * _dot(y, w_ref[...])
        merged = term if merged is None else merged + term
    o_ref[0] = x + _dot(merged.astype(BF16), wo_ref[...])


def _merge(x3, g, wg, bg, a3, cw, cb, lng, lnb, yb, yc, wa, wb, wc, wo, tm):
    b, s, _ = x3.shape
    per = tm // CONV_HALO
    row = lambda n: pl.BlockSpec((1, tm, n), lambda bi, i: (bi, i, 0))
    full = lambda a: pl.BlockSpec(a.shape, lambda bi, i: (0, 0), pipeline_mode=pl.Buffered(1))
    halo = pl.BlockSpec((1, CONV_HALO, CONV_DIM), lambda bi, i: (bi, jnp.maximum(i * per - 1, 0), 0))
    return pl.pallas_call(
        functools.partial(_merge_kernel, tm=tm),
        grid=(b, s // tm),
        in_specs=[row(D_MODEL), full(g), full(wg), full(bg), row(CONV_DIM), halo, full(cw), full(cb), full(lng),
                  full(lnb), row(MLA_HEADS * MLA_V), row(W_DV), full(wa), full(wb), full(wc), full(wo)],
        out_specs=row(D_MODEL),
        out_shape=jax.ShapeDtypeStruct((b, s, D_MODEL), F32),
        scratch_shapes=[pltpu.VMEM((tm + CONV_HALO, CONV_DIM), F32),
                        pltpu.VMEM((SUBLANES - 1, tm + CONV_HALO - SUBLANES, CONV_DIM), F32)],
        compiler_params=_cparams("parallel", "parallel"),
        name="merge",
    )(x3, g, wg, bg, a3, a3, cw, cb, lng, lnb, yb, yc, wa, wb, wc, wo)


def _ffn_kernel(x_ref, g_ref, wv_ref, wg_ref, cw_ref, cb_ref, wd_ref, gf_ref, o_ref,
                h_ref, acc_ref, buf_ref, carry_ref, *, tm, tiles_per_seq, final_norm):
    i = pl.program_id(0)
    x = x_ref[...]
    h_ref[...] = _rms(x, g_ref[...]).astype(BF16)
    acc_ref[...] = jnp.zeros_like(acc_ref)
    pad = SUBLANES

    @pl.when((i % tiles_per_seq) == 0)
    def _():
        carry_ref[...] = jnp.zeros_like(carry_ref)

    def up(c, slot):
        h = h_ref[...]
        buf_ref[slot, 0, pad:, :] = _dot(h, wv_ref[c])
        buf_ref[slot, 1, pad:, :] = _dot(h, wg_ref[c])

    def conv3(c, slot, half):
        buf = buf_ref.at[slot, half]
        buf[0:pad, :] = carry_ref[c, half]
        carry_ref[c, half] = buf[tm:tm + pad, :]
        w = cw_ref[c, half]
        return (w[0:1, :] * buf[pad - 2:pad - 2 + tm, :] + w[1:2, :] * buf[pad - 1:pad - 1 + tm, :]
                + w[2:3, :] * buf[pad:pad + tm, :] + cb_ref[c, half])

    def down(c, slot):
        uv = conv3(c, slot, 0)
        ug = conv3(c, slot, 1)
        act = (uv * jax.nn.sigmoid(uv) * ug).astype(BF16)
        acc_ref[...] += _dot(act, wd_ref[c])

    def pair(cc, carry):
        c = 2 * cc
        up(c + 1, 1)
        down(c, 0)
        up(c + 2, 0)
        down(c + 1, 1)
        return carry

    up(0, 0)
    n_pairs = (N_FF_CHUNK - 1) // 2
    lax.fori_loop(0, n_pairs, pair, 0)
    if (N_FF_CHUNK - 1) % 2 == 1:
        up(N_FF_CHUNK - 1, 1)
        down(N_FF_CHUNK - 2, 0)
        down(N_FF_CHUNK - 1, 1)
    else:
        down(N_FF_CHUNK - 1, 0)
    y = x + acc_ref[...]
    if final_norm:
        y = _rms(y, gf_ref[...])
    o_ref[...] = y


def _ffn(x2, g, wv, wg, cw, cb, wd, gf, tm, tiles_per_seq, final_norm):
    t = x2.shape[0]
    row = pl.BlockSpec((tm, D_MODEL), lambda i: (i, 0))
    full = lambda a: pl.BlockSpec(a.shape, lambda i: (0,) * a.ndim, pipeline_mode=pl.Buffered(1))
    return pl.pallas_call(
        functools.partial(_ffn_kernel, tm=tm, tiles_per_seq=tiles_per_seq, final_norm=final_norm),
        grid=(t // tm,),
        in_specs=[row, full(g), full(wv), full(wg), full(cw), full(cb), full(wd), full(gf)],
        out_specs=row,
        out_shape=jax.ShapeDtypeStruct((t, D_MODEL), F32),
        scratch_shapes=[pltpu.VMEM((tm, D_MODEL), BF16), pltpu.VMEM((tm, D_MODEL), F32),
                        pltpu.VMEM((2, 2, tm + SUBLANES, FF_CHUNK), F32),
                        pltpu.VMEM((N_FF_CHUNK, 2, SUBLANES, FF_CHUNK), F32)],
        compiler_params=_cparams("arbitrary"),
        name="ffn",
    )(x2, g, wv, wg, cw, cb, wd, gf)


def _rot_cols(w):
    half = w.shape[-1] // 2
    return jnp.concatenate([-w[..., half:], w[..., :half]], axis=-1)


def _prep_in_proj_weight(w_in_l):
    cuts = np.cumsum([W_GLU, MLA_Q_RANK, MLA_KV_RANK, MLA_ROPE, W_DQK, W_DQK, W_DV])
    glu, cq, ckv, kr, dq, dk, dv, gates = jnp.split(w_in_l, cuts, axis=1)
    z = lambda n: jnp.zeros((D_MODEL, n), w_in_l.dtype)
    tail = LANES - MLA_NOPE - MLA_ROPE
    w = jnp.concatenate([glu, cq, ckv, z(MLA_NOPE), kr, z(tail), z(MLA_NOPE), _rot_cols(kr), z(tail), dq, dk, dv],
                        axis=1)
    return w.astype(BF16), gates.astype(BF16)


def _prep_mla_weights(w_uq_l, w_ukv_l):
    dqk = MLA_NOPE + MLA_ROPE
    wq = w_uq_l.reshape(MLA_Q_RANK, MLA_HEADS, dqk)
    pad = jnp.zeros((MLA_Q_RANK, MLA_HEADS, LANES - dqk), wq.dtype)
    wqa = jnp.concatenate([wq, pad], axis=-1)
    wqb = jnp.concatenate([jnp.zeros_like(wq[..., :MLA_NOPE]), _rot_cols(wq[..., MLA_NOPE:]), pad], axis=-1)
    wkv = w_ukv_l.reshape(MLA_KV_RANK, MLA_HEADS, MLA_NOPE + MLA_V)
    zk = jnp.zeros((MLA_KV_RANK, MLA_HEADS, LANES - MLA_NOPE), wkv.dtype)
    wk = jnp.concatenate([wkv[..., :MLA_NOPE], zk], axis=-1)
    zv = jnp.zeros((MLA_KV_RANK, MLA_HEADS, LANES - MLA_V), wkv.dtype)
    v_even = jnp.concatenate([wkv[..., MLA_NOPE:], zv], axis=-1)
    v_odd = jnp.concatenate([zv, wkv[..., MLA_NOPE:]], axis=-1)
    odd = (jnp.arange(MLA_HEADS) % 2 == 1)[None, :, None]
    wv = jnp.where(odd, v_odd, v_even)
    flat = lambda a: a.reshape(a.shape[0], MLA_HEADS * LANES).astype(BF16)
    return flat(wqa), flat(wqb), flat(wk), flat(wv)


def _rope_tables(positions):
    half = MLA_ROPE // 2
    freqs = ROPE_THETA ** (-jnp.arange(half, dtype=F32) / half)
    ang = positions.astype(F32)[..., None] * freqs
    cos, sin = jnp.cos(ang), jnp.sin(ang)
    b, s = positions.shape
    ones = jnp.ones((b, s, MLA_NOPE), F32)
    zeros_n = jnp.zeros((b, s, MLA_NOPE), F32)
    zeros_t = jnp.zeros((b, s, LANES - MLA_NOPE - MLA_ROPE), F32)
    cos_t = jnp.concatenate([ones, cos, cos, zeros_t], axis=-1)
    sin_t = jnp.concatenate([zeros_n, sin, sin, zeros_t], axis=-1)
    return cos_t, sin_t


def _chunk_cols(w, n):
    return jnp.moveaxis(w.reshape(w.shape[0], n, FF_CHUNK), 1, 0)


def kernel(x, positions, rel_bias, norm_mix, w_in, gate_bias, conv_w, conv_b, conv_ln_g, conv_ln_b, w_conv_out,
           mla_q_norm, w_uq, mla_kv_norm, w_ukv, w_mla_out, diff_lam_q1, diff_lam_k1, diff_lam_q2, diff_lam_k2,
           diff_sub_norm, w_diff_out, w_out, norm_ffn, w_up, ffn_conv_w, ffn_conv_b, w_down, norm_final):
    b, s, d = x.shape
    depth = w_in.shape[0]
    assert d == D_MODEL
    tm = min(512, s)
    tm_ffn = min(1024, s)
    tq = min(512, s)
    assert s % tm == 0 and s % tm_ffn == 0 and s % tq == 0 and tm % CONV_HALO == 0
    t = b * s
    row2 = lambda a: a.reshape(1, -1)

    cos_t, sin_t = _rope_tables(positions)
    bias_tiles = _bias_tiles(rel_bias, tq)
    x2 = x.reshape(t, d)

    for l in range(depth):
        w_in_p, w_gate = _prep_in_proj_weight(w_in[l])
        a, small, dq, dk, dv = _in_proj(x2, row2(norm_mix[l]), w_in_p, tm)

        wqa, wqb, wk, wv = _prep_mla_weights(w_uq[l], w_ukv[l])
        q, k, v = _mla_prep(small.reshape(b, s, W_SMALL), cos_t, sin_t, row2(mla_q_norm[l]), row2(mla_kv_norm[l]),
                            wqa, wqb, wk, wv, tm)
        y_b = _mla_attn(q, k, v, tq)

        lam_init = 0.8 - 0.6 * math.exp(-0.3 * l)
        y_c = _diff_attn(dq.reshape(b, s, W_DQK), dk.reshape(b, s, W_DQK), dv.reshape(b, s, W_DV), bias_tiles,
                         row2(diff_lam_q1[l]), row2(diff_lam_k1[l]), row2(diff_lam_q2[l]), row2(diff_lam_k2[l]),
                         row2(diff_sub_norm[l]), lam_init, tq)

        x2 = _merge(x2.reshape(b, s, d), row2(norm_mix[l]), w_gate, row2(gate_bias[l]),
                    a.reshape(b, s, CONV_DIM), conv_w[l], row2(conv_b[l]), row2(conv_ln_g[l]), row2(conv_ln_b[l]),
                    y_b, y_c, w_conv_out[l].astype(BF16), w_mla_out[l].astype(BF16), w_diff_out[l].astype(BF16),
                    w_out[l].astype(BF16), tm).reshape(t, d)

        w_up_l = w_up[l].astype(BF16)
        cw = ffn_conv_w[l].reshape(FFN_CONV, 2, N_FF_CHUNK, FF_CHUNK).transpose(2, 1, 0, 3)
        cb = ffn_conv_b[l].reshape(2, N_FF_CHUNK, 1, FF_CHUNK).transpose(1, 0, 2, 3)
        x2 = _ffn(x2, row2(norm_ffn[l]), _chunk_cols(w_up_l[:, :D_FF], N_FF_CHUNK),
                  _chunk_cols(w_up_l[:, D_FF:], N_FF_CHUNK), cw, cb,
                  w_down[l].astype(BF16).reshape(N_FF_CHUNK, FF_CHUNK, D_MODEL), row2(norm_final),
                  tm_ffn, s // tm_ffn, l == depth - 1)
    return x2.reshape(b, s, d)
```

```python
import functools
import math

import numpy as np
import jax
import jax.numpy as jnp
from jax import lax
from jax.experimental import pallas as pl
from jax.experimental.pallas import tpu as pltpu

F32 = jnp.float32
BF16 = jnp.bfloat16

D_MODEL = 1024
CONV_DIM = 512
CONV_WIDTH = 31
MLA_HEADS = 8
MLA_NOPE = 64
MLA_ROPE = 32
MLA_V = 64
MLA_Q_RANK = 256
MLA_KV_RANK = 128
ROPE_THETA = 10000.0
DIFF_HEADS = 4
DIFF_HEAD = 64
DIFF_V = 2 * DIFF_HEAD
REL_BUCKETS = 32
REL_MAX_DIST = 128
D_FF = 2816
FFN_CONV = 3
N_BRANCH = 3
EPS = 1e-6

LANES = 128
SUBLANES = 8
VMEM_LIMIT = 56 * 1024 * 1024
NEG_BIG = -1e30
LOG2E = math.log2(math.e)

W_GLU = 2 * CONV_DIM
W_DQK = DIFF_HEADS * 2 * DIFF_HEAD
W_DV = DIFF_HEADS * DIFF_V
W_GATE = N_BRANCH * D_MODEL
W_SMALL = MLA_Q_RANK + MLA_KV_RANK + 2 * LANES
FF_CHUNK = 256
N_FF_CHUNK = D_FF // FF_CHUNK
CONV_HALO = 32
MLA_FLASH_UNROLL = 4
DIFF_FLASH_UNROLL = 2


def _cparams(*sem):
    return pltpu.CompilerParams(dimension_semantics=sem, vmem_limit_bytes=VMEM_LIMIT)


def _rms(x, g):
    return x * lax.rsqrt(jnp.mean(x * x, axis=-1, keepdims=True) + EPS) * g


def _dot(a, b):
    return jnp.dot(a, b, preferred_element_type=F32)


def _dot_nt(a, b):
    return lax.dot_general(a, b, (((1,), (1,)), ((), ())), preferred_element_type=F32)


def _mla_heads(sm, cos, sin, gq_ref, gkv_ref, wqa_ref, wqb_ref, wk_ref, wv_ref, q_ref, k_ref, v_ref):
    cq = sm[:, :MLA_Q_RANK]
    ckv = sm[:, MLA_Q_RANK:MLA_Q_RANK + MLA_KV_RANK]
    kr = sm[:, MLA_Q_RANK + MLA_KV_RANK:MLA_Q_RANK + MLA_KV_RANK + LANES]
    kr_rot = sm[:, MLA_Q_RANK + MLA_KV_RANK + LANES:]
    hq = _rms(cq, gq_ref[...]).astype(BF16)
    hkv = _rms(ckv, gkv_ref[...]).astype(BF16)
    qa = _dot(hq, wqa_ref[...])
    qb = _dot(hq, wqb_ref[...])
    kn = _dot(hkv, wk_ref[...])
    vv = _dot(hkv, wv_ref[...])
    kpe = kr * cos + kr_rot * sin
    lane = lax.broadcasted_iota(jnp.int32, (1, LANES), 1)
    scale = (MLA_NOPE + MLA_ROPE) ** -0.5 * LOG2E
    for h in range(MLA_HEADS):
        sl = slice(h * LANES, (h + 1) * LANES)
        q_ref[0, h] = ((qa[:, sl] * cos + qb[:, sl] * sin) * scale).astype(BF16)
        k_ref[0, h] = (kn[:, sl] + kpe).astype(BF16)
        one_lane = MLA_V if h % 2 == 0 else 0
        v_ref[0, h] = (vv[:, sl] + (lane == one_lane).astype(F32)).astype(BF16)


def _in_proj_kernel(x_ref, g_ref, w_ref, cos_ref, sin_ref, gq_ref, gkv_ref, wqa_ref, wqb_ref, wk_ref, wv_ref,
                    a_ref, dq_ref, dk_ref, dv_ref, q_ref, k_ref, v_ref):
    h = _rms(x_ref[...], g_ref[...]).astype(BF16)
    c0 = 0
    u = _dot(h, w_ref[:, c0:c0 + W_GLU])
    a_ref[...] = u[:, :CONV_DIM] * jax.nn.sigmoid(u[:, CONV_DIM:])
    c0 += W_GLU
    _mla_heads(_dot(h, w_ref[:, c0:c0 + W_SMALL]), cos_ref[...], sin_ref[...], gq_ref, gkv_ref,
               wqa_ref, wqb_ref, wk_ref, wv_ref, q_ref, k_ref, v_ref)
    c0 += W_SMALL
    dq_ref[...] = (_dot(h, w_ref[:, c0:c0 + W_DQK]) * (DIFF_HEAD ** -0.5 * LOG2E)).astype(BF16)
    c0 += W_DQK
    dk_ref[...] = _dot(h, w_ref[:, c0:c0 + W_DQK]).astype(BF16)
    c0 += W_DQK
    dv_ref[...] = _dot(h, w_ref[:, c0:c0 + W_DV]).astype(BF16)


def _in_proj(x2, g, w, cos_t, sin_t, gq, gkv, wqa, wqb, wk, wv, tm, b, s):
    t = x2.shape[0]
    tiles_per_seq = s // tm
    row = lambda n: pl.BlockSpec((tm, n), lambda i: (i, 0))
    full = lambda a: pl.BlockSpec(a.shape, lambda i: (0, 0), pipeline_mode=pl.Buffered(1))
    heads = pl.BlockSpec((1, MLA_HEADS, tm, LANES), lambda i: (i // tiles_per_seq, 0, i % tiles_per_seq, 0))
    return pl.pallas_call(
        _in_proj_kernel,
        grid=(t // tm,),
        in_specs=[row(D_MODEL), full(g), full(w), row(LANES), row(LANES), full(gq), full(gkv),
                  full(wqa), full(wqb), full(wk), full(wv)],
        out_specs=[row(CONV_DIM), row(W_DQK), row(W_DQK), row(W_DV), heads, heads, heads],
        out_shape=[jax.ShapeDtypeStruct((t, CONV_DIM), F32),
                   jax.ShapeDtypeStruct((t, W_DQK), BF16),
                   jax.ShapeDtypeStruct((t, W_DQK), BF16),
                   jax.ShapeDtypeStruct((t, W_DV), BF16)]
                  + [jax.ShapeDtypeStruct((b, MLA_HEADS, s, LANES), BF16)] * 3,
        compiler_params=_cparams("parallel"),
        name="in_proj",
    )(x2, g, w, cos_t, sin_t, gq, gkv, wqa, wqb, wk, wv)


def _score_tile(q, k, bias, s_ref, rmax_ref):
    s = _dot_nt(q, k) + bias
    s_ref[...] = s
    chunks = [s[:, c:c + LANES] for c in range(0, s.shape[1], LANES)]
    row_max = jnp.max(functools.reduce(jnp.maximum, chunks), axis=1, keepdims=True)
    rmax_ref[...] = jnp.broadcast_to(row_max, rmax_ref.shape)


def _softmax_update(s_ref, rmax_ref, v, m_ref, acc_ref):
    m_old = m_ref[...]
    m_new = jnp.maximum(m_old, rmax_ref[...])
    alpha = jnp.exp2(m_old - m_new)
    p = jnp.concatenate([jnp.exp2(s_ref[:, c:c + LANES] - m_new) for c in range(0, s_ref.shape[1], LANES)], axis=1)
    pv = _dot(p.astype(BF16), v)
    acc_ref[...] = jnp.concatenate([alpha] * (acc_ref.shape[1] // LANES), axis=1) * acc_ref[...] + pv
    m_ref[...] = m_new


def _causal_tiles(nq):
    return [(qi, kj) for qi in range(nq) for kj in range(qi + 1)]


def _tile_tables(nq):
    tiles = np.asarray(_causal_tiles(nq), dtype=np.int32)
    return jnp.asarray(tiles[:, 0]), jnp.asarray(tiles[:, 1])


def _causal_flash(streams, qtab, ktab, n_tiles, tq, bias_ref, s_buf, rmax_buf, unroll):
    n_kinds = bias_ref.shape[0]

    def blk(ref, j):
        return ref[pl.ds(pl.multiple_of(j * tq, tq), tq), :]

    def scores(t, dst):
        qi, kj = qtab[t], ktab[t]
        bias = bias_ref[jnp.minimum(qi - kj, n_kinds - 1)]
        for si, (q_ref, k_ref, _, _, _) in enumerate(streams):
            _score_tile(blk(q_ref, qi), blk(k_ref, kj), bias, s_buf.at[dst, si], rmax_buf.at[dst, si])

    def update(t, src):
        qi, kj = qtab[t], ktab[t]
        for si, (_, _, v_ref, m_ref, acc_ref) in enumerate(streams):
            _softmax_update(s_buf.at[src, si], rmax_buf.at[src, si], blk(v_ref, kj), m_ref.at[qi], acc_ref.at[qi])

    def group(jj, carry):
        for u in range(unroll):
            t = unroll * jj + u
            scores(t + 1, (u + 1) % 2)
            update(t, u % 2)
        return carry

    for _, _, _, m_ref, acc_ref in streams:
        m_ref[...] = jnp.full(m_ref.shape, NEG_BIG, F32)
        acc_ref[...] = jnp.zeros(acc_ref.shape, F32)
    scores(0, 0)
    assert unroll % 2 == 0
    n_groups = (n_tiles - 1) // unroll
    lax.fori_loop(0, n_groups, group, 0)
    for t in range(n_groups * unroll, n_tiles - 1):
        scores(t + 1, (t + 1) % 2)
        update(t, t % 2)
    update(n_tiles - 1, (n_tiles - 1) % 2)


def _mla_attn_kernel(qtab, ktab, q_ref, k_ref, v_ref, o_ref, m_ref, acc_ref, s_buf, rmax_buf, mask_ref,
                     *, tq, nq):
    row = lax.broadcasted_iota(jnp.int32, (tq, tq), 0)
    col = lax.broadcasted_iota(jnp.int32, (tq, tq), 1)
    mask_ref[0] = jnp.where(row >= col, 0.0, NEG_BIG).astype(F32)
    mask_ref[1] = jnp.zeros((tq, tq), F32)
    streams = [(q_ref.at[0, hh], k_ref.at[0, hh], v_ref.at[0, hh], m_ref.at[hh], acc_ref.at[hh])
               for hh in range(2)]
    _causal_flash(streams, qtab, ktab, nq * (nq + 1) // 2, tq, mask_ref, s_buf, rmax_buf, MLA_FLASH_UNROLL)
    lane = lax.broadcasted_iota(jnp.int32, (1, LANES), 1)
    for qi in range(nq):
        outs = []
        for hh in range(2):
            acc = acc_ref[hh, qi]
            denom_lane = MLA_V if hh == 0 else 0
            denom = jnp.sum(jnp.where(lane == denom_lane, acc, 0.0), axis=1, keepdims=True)
            outs.append(acc / denom)
        o_ref[0, qi * tq:(qi + 1) * tq, :] = jnp.where(lane < MLA_V, outs[0], outs[1]).astype(BF16)


def _mla_attn(q, k, v, tq):
    b, nh, s, _ = q.shape
    nq = s // tq
    qtab, ktab = _tile_tables(nq)
    smem = pl.BlockSpec(memory_space=pltpu.SMEM)
    seq = pl.BlockSpec((1, 2, s, LANES), lambda bi, hp: (bi, hp, 0, 0))
    return pl.pallas_call(
        functools.partial(_mla_attn_kernel, tq=tq, nq=nq),
        grid=(b, nh // 2),
        in_specs=[smem, smem, seq, seq, seq],
        out_specs=pl.BlockSpec((1, s, LANES), lambda bi, hp: (bi, 0, hp)),
        out_shape=jax.ShapeDtypeStruct((b, s, nh * MLA_V), BF16),
        scratch_shapes=[pltpu.VMEM((2, nq, tq, LANES), F32), pltpu.VMEM((2, nq, tq, LANES), F32),
                        pltpu.VMEM((2, 2, tq, tq), F32), pltpu.VMEM((2, 2, tq, LANES), F32),
                        pltpu.VMEM((2, tq, tq), F32)],
        compiler_params=_cparams("parallel", "parallel"),
        name="mla_attn",
    )(qtab, ktab, q, k, v)


def _bucket_upper_bounds():
    n = np.arange(0, 4 * REL_MAX_DIST, dtype=np.int32)
    max_exact = REL_BUCKETS // 2
    nf = np.maximum(n, 1).astype(np.float32)
    large = max_exact + (np.log(nf / np.float32(max_exact)) / np.float32(math.log(REL_MAX_DIST / max_exact))
                         * np.float32(REL_BUCKETS - max_exact)).astype(np.int32)
    large = np.minimum(large, REL_BUCKETS - 1)
    bucket = np.where(n < max_exact, n, large)
    assert np.all(np.diff(bucket) >= 0) and bucket[-1] == REL_BUCKETS - 1
    return [int(np.max(n[bucket == b])) for b in range(REL_BUCKETS - 1)]


def _bias_tile_kernel(tbl_ref, o_ref, *, tq, bounds):
    h = pl.program_id(0)
    kind = pl.program_id(1)
    row = lax.broadcasted_iota(jnp.int32, (tq, tq), 0)
    col = lax.broadcasted_iota(jnp.int32, (tq, tq), 1)
    n = row - col + kind * tq
    far = tbl_ref[(REL_BUCKETS - 1) * DIFF_HEADS + h]
    r = jnp.zeros((tq, tq), F32)
    for bkt in range(REL_BUCKETS - 2, -1, -1):
        r = jnp.where(n <= bounds[bkt], (tbl_ref[bkt * DIFF_HEADS + h] - far) * LOG2E, r)
    o_ref[0, 0] = jnp.where(n >= 0, r, NEG_BIG)


def _bias_tiles(rel_bias, tq):
    bounds = _bucket_upper_bounds()
    assert bounds[-1] < tq, "bias must be constant from the second key block before the diagonal on"
    return pl.pallas_call(
        functools.partial(_bias_tile_kernel, tq=tq, bounds=bounds),
        grid=(DIFF_HEADS, 3),
        in_specs=[pl.BlockSpec(memory_space=pltpu.SMEM)],
        out_specs=pl.BlockSpec((1, 1, tq, tq), lambda h, kd: (h, kd, 0, 0)),
        out_shape=jax.ShapeDtypeStruct((DIFF_HEADS, 3, tq, tq), F32),
        compiler_params=_cparams("parallel", "parallel"),
        name="rel_bias_tiles",
    )(rel_bias.reshape(-1))


def _diff_attn_kernel(qtab, ktab, q_ref, k_ref, v_ref, bias_ref, lq1_ref, lk1_ref, lq2_ref, lk2_ref, gsub_ref,
                      o_ref, m_ref, acc_ref, s_buf, rmax_buf, qm_ref, vx_ref, *, tq, nq, lam_init):
    lam = (jnp.exp(jnp.sum(lq1_ref[...] * lk1_ref[...], axis=1, keepdims=True))
           - jnp.exp(jnp.sum(lq2_ref[...] * lk2_ref[...], axis=1, keepdims=True)) + lam_init)
    q = q_ref[0]
    lane = lax.broadcasted_iota(jnp.int32, (1, LANES), 1)
    zero = jnp.zeros_like(q)
    qm_ref[0] = jnp.where(lane < DIFF_HEAD, q, zero)
    qm_ref[1] = jnp.where(lane >= DIFF_HEAD, q, zero)
    vx_ref[:, :DIFF_V] = v_ref[0]
    vx_ref[:, DIFF_V:] = jnp.broadcast_to((lane == 0).astype(BF16), (vx_ref.shape[0], LANES))
    streams = [(qm_ref.at[mp], k_ref.at[0], vx_ref, m_ref.at[mp], acc_ref.at[mp]) for mp in range(2)]
    _causal_flash(streams, qtab, ktab, nq * (nq + 1) // 2, tq, bias_ref.at[0], s_buf, rmax_buf, DIFF_FLASH_UNROLL)
    for qi in range(nq):
        outs = []
        for mp in range(2):
            acc = acc_ref[mp, qi]
            outs.append(acc[:, :DIFF_V] / jnp.sum(acc[:, DIFF_V:], axis=1, keepdims=True))
        o = outs[0] - lam * outs[1]
        o_ref[0, qi * tq:(qi + 1) * tq, :] = (_rms(o, gsub_ref[...]) * (1.0 - lam_init)).astype(BF16)


def _diff_attn(dq, dk, dv, bias_tiles, lq1, lk1, lq2, lk2, gsub, lam_init, tq):
    b, s, _ = dq.shape
    nq = s // tq
    qtab, ktab = _tile_tables(nq)
    smem = pl.BlockSpec(memory_space=pltpu.SMEM)
    vec = lambda a: pl.BlockSpec(a.shape, lambda bi, h: (0, 0))
    seq = pl.BlockSpec((1, s, LANES), lambda bi, h: (bi, 0, h))
    n_kinds = bias_tiles.shape[1]
    return pl.pallas_call(
        functools.partial(_diff_attn_kernel, tq=tq, nq=nq, lam_init=lam_init),
        grid=(b, DIFF_HEADS),
        in_specs=[smem, smem, seq, seq, seq,
                  pl.BlockSpec((1, n_kinds, tq, tq), lambda bi, h: (h, 0, 0, 0)),
                  vec(lq1), vec(lk1), vec(lq2), vec(lk2), vec(gsub)],
        out_specs=seq,
        out_shape=jax.ShapeDtypeStruct((b, s, W_DV), BF16),
        scratch_shapes=[pltpu.VMEM((2, nq, tq, LANES), F32), pltpu.VMEM((2, nq, tq, 2 * LANES), F32),
                        pltpu.VMEM((2, 2, tq, tq), F32), pltpu.VMEM((2, 2, tq, LANES), F32),
                        pltpu.VMEM((2, s, LANES), BF16), pltpu.VMEM((s, 2 * LANES), BF16)],
        compiler_params=_cparams("parallel", "parallel"),
        name="diff_attn",
    )(qtab, ktab, dq, dk, dv, bias_tiles, lq1, lk1, lq2, lk2, gsub)


def _conv_ln_silu(a, halo, w_ref, b_ref, lng_ref, lnb_ref, buf_ref, shift_ref, tm):
    buf_ref[0:CONV_HALO, :] = halo
    buf_ref[CONV_HALO:, :] = a
    n_rows = tm + CONV_HALO - SUBLANES
    for r in range(1, SUBLANES):
        shift_ref[r - 1] = buf_ref[r:r + n_rows, :]
    base = CONV_HALO - (CONV_WIDTH - 1)
    y = jnp.zeros((tm, CONV_DIM), F32) + b_ref[...]
    for j in range(CONV_WIDTH):
        grp, r = divmod(base + j, SUBLANES)
        lo = grp * SUBLANES
        tap = buf_ref[lo:lo + tm, :] if r == 0 else shift_ref[r - 1, lo:lo + tm, :]
        y = y + w_ref[j:j + 1, :] * tap
    mu = jnp.mean(y, axis=-1, keepdims=True)
    var = jnp.mean(jnp.square(y - mu), axis=-1, keepdims=True)
    z = (y - mu) * lax.rsqrt(var + EPS) * lng_ref[...] + lnb_ref[...]
    return (z * jax.nn.sigmoid(z)).astype(BF16)


def _merge_kernel(x_ref, g_ref, wg_ref, bg_ref, a_ref, halo_ref, cw_ref, cb_ref, lng_ref, lnb_ref,
                  yb_ref, yc_ref, wa_ref, wb_ref, wc_ref, wo_ref, o_ref, buf_ref, shift_ref, *, tm, tiles_per_seq):
    x = x_ref[...]
    h = _rms(x, g_ref[...]).astype(BF16)
    halo = halo_ref[...]
    starts_seq = (pl.program_id(0) % tiles_per_seq) == 0
    ya = _conv_ln_silu(a_ref[...], jnp.where(starts_seq, jnp.zeros_like(halo), halo), cw_ref, cb_ref, lng_ref,
                       lnb_ref, buf_ref, shift_ref, tm)
    merged = None
    for br, (y, w_ref) in enumerate(((ya, wa_ref), (yb_ref[...], wb_ref), (yc_ref[...], wc_ref))):
        sl = slice(br * D_MODEL, (br + 1) * D_MODEL)
        gate = jax.nn.sigmoid(_dot(h, wg_ref[:, sl]) + bg_ref[:, sl])
        term = gate * _dot(y, w_ref[...])
        merged = term if merged is None else merged + term
    o_ref[...] = x + _dot(merged.astype(BF16), wo_ref[...])


def _merge(x2, g, wg, bg, a2, cw, cb, lng, lnb, yb, yc, wa, wb, wc, wo, tm, tiles_per_seq):
    t = x2.shape[0]
    n = t // tm
    per = tm // CONV_HALO
    row = lambda w: pl.BlockSpec((tm, w), lambda i: (i, 0))
    full = lambda a: pl.BlockSpec(a.shape, lambda i: (0, 0), pipeline_mode=pl.Buffered(1))
    halo = pl.BlockSpec((CONV_HALO, CONV_DIM), lambda i: (jnp.maximum(i * per - 1, 0), 0))
    return pl.pallas_call(
        functools.partial(_merge_kernel, tm=tm, tiles_per_seq=tiles_per_seq),
        grid=(n,),
        in_specs=[row(D_MODEL), full(g), full(wg), full(bg), row(CONV_DIM), halo, full(cw), full(cb), full(lng),
                  full(lnb), row(MLA_HEADS * MLA_V), row(W_DV), full(wa), full(wb), full(wc), full(wo)],
        out_specs=row(D_MODEL),
        out_shape=jax.ShapeDtypeStruct((t, D_MODEL), F32),
        scratch_shapes=[pltpu.VMEM((tm + CONV_HALO, CONV_DIM), F32),
                        pltpu.VMEM((SUBLANES - 1, tm + CONV_HALO - SUBLANES, CONV_DIM), F32)],
        compiler_params=_cparams("parallel"),
        name="merge",
    )(x2, g, wg, bg, a2, a2, cw, cb, lng, lnb, yb, yc, wa, wb, wc, wo)


def _ffn_kernel(x_ref, g_ref, wv_ref, wg_ref, cw_ref, cb_ref, wd_ref, gf_ref, o_ref,
                h_ref, acc_ref, buf_ref, carry_ref, *, tm, tiles_per_seq, final_norm):
    i = pl.program_id(0)
    x = x_ref[...]
    h_ref[...] = _rms(x, g_ref[...]).astype(BF16)
    acc_ref[...] = jnp.zeros_like(acc_ref)
    pad = SUBLANES

    @pl.when((i % tiles_per_seq) == 0)
    def _():
        carry_ref[...] = jnp.zeros_like(carry_ref)

    def up(c, slot):
        h = h_ref[...]
        buf_ref[slot, 0, pad:, :] = _dot(h, wv_ref[c])
        buf_ref[slot, 1, pad:, :] = _dot(h, wg_ref[c])

    def conv3(c, slot, half):
        buf = buf_ref.at[slot, half]
        buf[0:pad, :] = carry_ref[c, half]
        carry_ref[c, half] = buf[tm:tm + pad, :]
        w = cw_ref[c, half]
        return (w[0:1, :] * buf[pad - 2:pad - 2 + tm, :] + w[1:2, :] * buf[pad - 1:pad - 1 + tm, :]
                + w[2:3, :] * buf[pad:pad + tm, :] + cb_ref[c, half])

    def down(c, slot):
        uv = conv3(c, slot, 0)
        ug = conv3(c, slot, 1)
        act = (uv * jax.nn.sigmoid(uv) * ug).astype(BF16)
        acc_ref[...] += _dot(act, wd_ref[c])

    def pair(cc, carry):
        c = 2 * cc
        up(c + 1, 1)
        down(c, 0)
        up(c + 2, 0)
        down(c + 1, 1)
        return carry

    up(0, 0)
    n_pairs = (N_FF_CHUNK - 1) // 2
    lax.fori_loop(0, n_pairs, pair, 0)
    if (N_FF_CHUNK - 1) % 2 == 1:
        up(N_FF_CHUNK - 1, 1)
        down(N_FF_CHUNK - 2, 0)
        down(N_FF_CHUNK - 1, 1)
    else:
        down(N_FF_CHUNK - 1, 0)
    y = x + acc_ref[...]
    if final_norm:
        y = _rms(y, gf_ref[...])
    o_ref[...] = y


def _ffn(x2, g, wv, wg, cw, cb, wd, gf, tm, tiles_per_seq, final_norm):
    t = x2.shape[0]
    row = pl.BlockSpec((tm, D_MODEL), lambda i: (i, 0))
    full = lambda a: pl.BlockSpec(a.shape, lambda i: (0,) * a.ndim, pipeline_mode=pl.Buffered(1))
    return pl.pallas_call(
        functools.partial(_ffn_kernel, tm=tm, tiles_per_seq=tiles_per_seq, final_norm=final_norm),
        grid=(t // tm,),
        in_specs=[row, full(g), full(wv), full(wg), full(cw), full(cb), full(wd), full(gf)],
        out_specs=row,
        out_shape=jax.ShapeDtypeStruct((t, D_MODEL), F32),
        scratch_shapes=[pltpu.VMEM((tm, D_MODEL), BF16), pltpu.VMEM((tm, D_MODEL), F32),
                        pltpu.VMEM((2, 2, tm + SUBLANES, FF_CHUNK), F32),
                        pltpu.VMEM((N_FF_CHUNK, 2, SUBLANES, FF_CHUNK), F32)],
        compiler_params=_cparams("arbitrary"),
        name="ffn",
    )(x2, g, wv, wg, cw, cb, wd, gf)


def _rot_cols(w):
    half = w.shape[-1] // 2
    return jnp.concatenate([-w[..., half:], w[..., :half]], axis=-1)


def _prep_in_proj_weight(w_in_l):
    cuts = np.cumsum([W_GLU, MLA_Q_RANK, MLA_KV_RANK, MLA_ROPE, W_DQK, W_DQK, W_DV])
    glu, cq, ckv, kr, dq, dk, dv, gates = jnp.split(w_in_l, cuts, axis=1)
    z = lambda n: jnp.zeros((D_MODEL, n), w_in_l.dtype)
    tail = LANES - MLA_NOPE - MLA_ROPE
    w = jnp.concatenate([glu, cq, ckv, z(MLA_NOPE), kr, z(tail), z(MLA_NOPE), _rot_cols(kr), z(tail), dq, dk, dv],
                        axis=1)
    return w.astype(BF16), gates.astype(BF16)


def _prep_mla_weights(w_uq_l, w_ukv_l):
    dqk = MLA_NOPE + MLA_ROPE
    wq = w_uq_l.reshape(MLA_Q_RANK, MLA_HEADS, dqk)
    pad = jnp.zeros((MLA_Q_RANK, MLA_HEADS, LANES - dqk), wq.dtype)
    wqa = jnp.concatenate([wq, pad], axis=-1)
    wqb = jnp.concatenate([jnp.zeros_like(wq[..., :MLA_NOPE]), _rot_cols(wq[..., MLA_NOPE:]), pad], axis=-1)
    wkv = w_ukv_l.reshape(MLA_KV_RANK, MLA_HEADS, MLA_NOPE + MLA_V)
    zk = jnp.zeros((MLA_KV_RANK, MLA_HEADS, LANES - MLA_NOPE), wkv.dtype)
    wk = jnp.concatenate([wkv[..., :MLA_NOPE], zk], axis=-1)
    zv = jnp.zeros((MLA_KV_RANK, MLA_HEADS, LANES - MLA_V), wkv.dtype)
    v_even = jnp.concatenate([wkv[..., MLA_NOPE:], zv], axis=-1)
    v_odd = jnp.concatenate([zv, wkv[..., MLA_NOPE:]], axis=-1)
    odd = (jnp.arange(MLA_HEADS) % 2 == 1)[None, :, None]
    wv = jnp.where(odd, v_odd, v_even)
    flat = lambda a: a.reshape(a.shape[0], MLA_HEADS * LANES).astype(BF16)
    return flat(wqa), flat(wqb), flat(wk), flat(wv)


def _rope_tables(positions):
    half = MLA_ROPE // 2
    freqs = ROPE_THETA ** (-jnp.arange(half, dtype=F32) / half)
    ang = positions.astype(F32)[..., None] * freqs
    cos, sin = jnp.cos(ang), jnp.sin(ang)
    b, s = positions.shape
    ones = jnp.ones((b, s, MLA_NOPE), F32)
    zeros_n = jnp.zeros((b, s, MLA_NOPE), F32)
    zeros_t = jnp.zeros((b, s, LANES - MLA_NOPE - MLA_ROPE), F32)
    cos_t = jnp.concatenate([ones, cos, cos, zeros_t], axis=-1)
    sin_t = jnp.concatenate([zeros_n, sin, sin, zeros_t], axis=-1)
    return cos_t, sin_t


def _chunk_cols(w, n):
    return jnp.moveaxis(w.reshape(w.shape[0], n, FF_CHUNK), 1, 0)


def kernel(x, positions, rel_bias, norm_mix, w_in, gate_bias, conv_w, conv_b, conv_ln_g, conv_ln_b, w_conv_out,
           mla_q_norm, w_uq, mla_kv_norm, w_ukv, w_mla_out, diff_lam_q1, diff_lam_k1, diff_lam_q2, diff_lam_k2,
           diff_sub_norm, w_diff_out, w_out, norm_ffn, w_up, ffn_conv_w, ffn_conv_b, w_down, norm_final):
    b, s, d = x.shape
    depth = w_in.shape[0]
    assert d == D_MODEL
    tm = min(512, s)
    tm_ffn = min(1024, s)
    tq = min(512, s)
    assert s % tm == 0 and s % tm_ffn == 0 and s % tq == 0 and tm % CONV_HALO == 0
    t = b * s
    row2 = lambda a: a.reshape(1, -1)

    cos_t, sin_t = (tbl.reshape(t, LANES) for tbl in _rope_tables(positions))
    bias_tiles = _bias_tiles(rel_bias, tq)
    x2 = x.reshape(t, d)

    for l in range(depth):
        w_in_p, w_gate = _prep_in_proj_weight(w_in[l])
        wqa, wqb, wk, wv = _prep_mla_weights(w_uq[l], w_ukv[l])
        a, dq, dk, dv, q, k, v = _in_proj(x2, row2(norm_mix[l]), w_in_p, cos_t, sin_t, row2(mla_q_norm[l]),
                                          row2(mla_kv_norm[l]), wqa, wqb, wk, wv, tm, b, s)
        y_b = _mla_attn(q, k, v, tq)

        lam_init = 0.8 - 0.6 * math.exp(-0.3 * l)
        y_c = _diff_attn(dq.reshape(b, s, W_DQK), dk.reshape(b, s, W_DQK), dv.reshape(b, s, W_DV), bias_tiles,
                         row2(diff_lam_q1[l]), row2(diff_lam_k1[l]), row2(diff_lam_q2[l]), row2(diff_lam_k2[l]),
                         row2(diff_sub_norm[l]), lam_init, tq)

        x2 = _merge(x2, row2(norm_mix[l]), w_gate, row2(gate_bias[l]), a, conv_w[l], row2(conv_b[l]),
                    row2(conv_ln_g[l]), row2(conv_ln_b[l]), y_b.reshape(t, MLA_HEADS * MLA_V), y_c.reshape(t, W_DV),
                    w_conv_out[l].astype(BF16), w_mla_out[l].astype(BF16), w_diff_out[l].astype(BF16),
                    w_out[l].astype(BF16), tm, s // tm)

        w_up_l = w_up[l].astype(BF16)
        cw = ffn_conv_w[l].reshape(FFN_CONV, 2, N_FF_CHUNK, FF_CHUNK).transpose(2, 1, 0, 3)
        cb = ffn_conv_b[l].reshape(2, N_FF_CHUNK, 1, FF_CHUNK).transpose(1, 0, 2, 3)
        x2 = _ffn(x2, row2(norm_ffn[l]), _chunk_cols(w_up_l[:, :D_FF], N_FF_CHUNK),
                  _chunk_cols(w_up_l[:, D_FF:], N_FF_CHUNK), cw, cb,
                  w_down[l].astype(BF16).reshape(N_FF_CHUNK, FF_CHUNK, D_MODEL), row2(norm_final),
                  tm_ffn, s // tm_ffn, l == depth - 1)
    return x2.reshape(b, s, d)
```

```python
import functools
import math

import numpy as np
import jax
import jax.numpy as jnp
from jax import lax
from jax.experimental import pallas as pl
from jax.experimental.pallas import tpu as pltpu

F32 = jnp.float32
BF16 = jnp.bfloat16

D_MODEL = 1024
CONV_DIM = 512
CONV_WIDTH = 31
MLA_HEADS = 8
MLA_NOPE = 64
MLA_ROPE = 32
MLA_V = 64
MLA_Q_RANK = 256
MLA_KV_RANK = 128
ROPE_THETA = 10000.0
DIFF_HEADS = 4
DIFF_HEAD = 64
DIFF_V = 2 * DIFF_HEAD
REL_BUCKETS = 32
REL_MAX_DIST = 128
D_FF = 2816
FFN_CONV = 3
N_BRANCH = 3
EPS = 1e-6

LANES = 128
SUBLANES = 8
VMEM_LIMIT = 56 * 1024 * 1024
NEG_BIG = -1e30
LOG2E = math.log2(math.e)

W_GLU = 2 * CONV_DIM
W_DQK = DIFF_HEADS * 2 * DIFF_HEAD
W_DV = DIFF_HEADS * DIFF_V
W_GATE = N_BRANCH * D_MODEL
W_SMALL = MLA_Q_RANK + MLA_KV_RANK + 2 * LANES
FF_CHUNK = 256
N_FF_CHUNK = D_FF // FF_CHUNK
CONV_HALO = 32
MLA_FLASH_UNROLL = 4
DIFF_FLASH_UNROLL = 2


def _cparams(*sem):
    return pltpu.CompilerParams(dimension_semantics=sem, vmem_limit_bytes=VMEM_LIMIT)


def _rms(x, g):
    return x * lax.rsqrt(jnp.mean(x * x, axis=-1, keepdims=True) + EPS) * g


def _dot(a, b):
    return jnp.dot(a, b, preferred_element_type=F32)


def _dot_nt(a, b):
    return lax.dot_general(a, b, (((1,), (1,)), ((), ())), preferred_element_type=F32)


def _layer_block(stacked, l):
    zeros = (0,) * (stacked.ndim - 1)
    return pl.BlockSpec((None,) + stacked.shape[1:], lambda *_: (l,) + zeros, pipeline_mode=pl.Buffered(1))


def _mla_heads(sm, cos, sin, gq_ref, gkv_ref, wqa_ref, wqb_ref, wk_ref, wv_ref, q_ref, k_ref, v_ref):
    cq = sm[:, :MLA_Q_RANK]
    ckv = sm[:, MLA_Q_RANK:MLA_Q_RANK + MLA_KV_RANK]
    kr = sm[:, MLA_Q_RANK + MLA_KV_RANK:MLA_Q_RANK + MLA_KV_RANK + LANES]
    kr_rot = sm[:, MLA_Q_RANK + MLA_KV_RANK + LANES:]
    hq = _rms(cq, gq_ref[...]).astype(BF16)
    hkv = _rms(ckv, gkv_ref[...]).astype(BF16)
    qa = _dot(hq, wqa_ref[...])
    qb = _dot(hq, wqb_ref[...])
    kn = _dot(hkv, wk_ref[...])
    vv = _dot(hkv, wv_ref[...])
    kpe = kr * cos + kr_rot * sin
    lane = lax.broadcasted_iota(jnp.int32, (1, LANES), 1)
    scale = (MLA_NOPE + MLA_ROPE) ** -0.5 * LOG2E
    for h in range(MLA_HEADS):
        sl = slice(h * LANES, (h + 1) * LANES)
        q_ref[0, h] = ((qa[:, sl] * cos + qb[:, sl] * sin) * scale).astype(BF16)
        k_ref[0, h] = (kn[:, sl] + kpe).astype(BF16)
        one_lane = MLA_V if h % 2 == 0 else 0
        v_ref[0, h] = (vv[:, sl] + (lane == one_lane).astype(F32)).astype(BF16)


def _in_proj_kernel(x_ref, g_ref, w_ref, cos_ref, sin_ref, gq_ref, gkv_ref, wqa_ref, wqb_ref, wk_ref, wv_ref,
                    a_ref, dq_ref, dk_ref, dv_ref, q_ref, k_ref, v_ref):
    h = _rms(x_ref[...], g_ref[...]).astype(BF16)
    c0 = 0
    u = _dot(h, w_ref[:, c0:c0 + W_GLU])
    a_ref[...] = u[:, :CONV_DIM] * jax.nn.sigmoid(u[:, CONV_DIM:])
    c0 += W_GLU
    _mla_heads(_dot(h, w_ref[:, c0:c0 + W_SMALL]), cos_ref[...], sin_ref[...], gq_ref, gkv_ref,
               wqa_ref, wqb_ref, wk_ref, wv_ref, q_ref, k_ref, v_ref)
    c0 += W_SMALL
    dq_ref[...] = (_dot(h, w_ref[:, c0:c0 + W_DQK]) * (DIFF_HEAD ** -0.5 * LOG2E)).astype(BF16)
    c0 += W_DQK
    dk_ref[...] = _dot(h, w_ref[:, c0:c0 + W_DQK]).astype(BF16)
    c0 += W_DQK
    dv_ref[...] = _dot(h, w_ref[:, c0:c0 + W_DV]).astype(BF16)


def _in_proj(l, x2, cos_t, sin_t, layer_ops, tm, b, s):
    t = x2.shape[0]
    tiles_per_seq = s // tm
    row = lambda n: pl.BlockSpec((tm, n), lambda i: (i, 0))
    heads = pl.BlockSpec((1, MLA_HEADS, tm, LANES), lambda i: (i // tiles_per_seq, 0, i % tiles_per_seq, 0))
    g, w, gq, gkv, wqa, wqb, wk, wv = layer_ops
    lb = lambda a: _layer_block(a, l)
    return pl.pallas_call(
        _in_proj_kernel,
        grid=(t // tm,),
        in_specs=[row(D_MODEL), lb(g), lb(w), row(LANES), row(LANES), lb(gq), lb(gkv),
                  lb(wqa), lb(wqb), lb(wk), lb(wv)],
        out_specs=[row(CONV_DIM), row(W_DQK), row(W_DQK), row(W_DV), heads, heads, heads],
        out_shape=[jax.ShapeDtypeStruct((t, CONV_DIM), F32),
                   jax.ShapeDtypeStruct((t, W_DQK), BF16),
                   jax.ShapeDtypeStruct((t, W_DQK), BF16),
                   jax.ShapeDtypeStruct((t, W_DV), BF16)]
                  + [jax.ShapeDtypeStruct((b, MLA_HEADS, s, LANES), BF16)] * 3,
        compiler_params=_cparams("parallel"),
        name="in_proj",
    )(x2, g, w, cos_t, sin_t, gq, gkv, wqa, wqb, wk, wv)


def _score_tile(q, k, bias, s_ref, rmax_ref):
    s = _dot_nt(q, k) + bias
    s_ref[...] = s
    chunks = [s[:, c:c + LANES] for c in range(0, s.shape[1], LANES)]
    row_max = jnp.max(functools.reduce(jnp.maximum, chunks), axis=1, keepdims=True)
    rmax_ref[...] = jnp.broadcast_to(row_max, rmax_ref.shape)


def _softmax_update(s_ref, rmax_ref, v, m_ref, acc_ref):
    m_old = m_ref[...]
    m_new = jnp.maximum(m_old, rmax_ref[...])
    alpha = jnp.exp2(m_old - m_new)
    p = jnp.concatenate([jnp.exp2(s_ref[:, c:c + LANES] - m_new) for c in range(0, s_ref.shape[1], LANES)], axis=1)
    pv = _dot(p.astype(BF16), v)
    acc_ref[...] = jnp.concatenate([alpha] * (acc_ref.shape[1] // LANES), axis=1) * acc_ref[...] + pv
    m_ref[...] = m_new


def _causal_tiles(nq):
    return [(qi, kj) for qi in range(nq) for kj in range(qi + 1)]


def _tile_tables(nq):
    tiles = np.asarray(_causal_tiles(nq), dtype=np.int32)
    return jnp.asarray(tiles[:, 0]), jnp.asarray(tiles[:, 1])


def _causal_flash(streams, qtab, ktab, n_tiles, tq, bias_ref, s_buf, rmax_buf, unroll):
    n_kinds = bias_ref.shape[0]

    def blk(ref, j):
        return ref[pl.ds(pl.multiple_of(j * tq, tq), tq), :]

    def scores(t, dst):
        qi, kj = qtab[t], ktab[t]
        bias = bias_ref[jnp.minimum(qi - kj, n_kinds - 1)]
        for si, (q_ref, k_ref, _, _, _) in enumerate(streams):
            _score_tile(blk(q_ref, qi), blk(k_ref, kj), bias, s_buf.at[dst, si], rmax_buf.at[dst, si])

    def update(t, src):
        qi, kj = qtab[t], ktab[t]
        for si, (_, _, v_ref, m_ref, acc_ref) in enumerate(streams):
            _softmax_update(s_buf.at[src, si], rmax_buf.at[src, si], blk(v_ref, kj), m_ref.at[qi], acc_ref.at[qi])

    def group(jj, carry):
        for u in range(unroll):
            t = unroll * jj + u
            scores(t + 1, (u + 1) % 2)
            update(t, u % 2)
        return carry

    for _, _, _, m_ref, acc_ref in streams:
        m_ref[...] = jnp.full(m_ref.shape, NEG_BIG, F32)
        acc_ref[...] = jnp.zeros(acc_ref.shape, F32)
    scores(0, 0)
    assert unroll % 2 == 0
    n_groups = (n_tiles - 1) // unroll
    lax.fori_loop(0, n_groups, group, 0)
    for t in range(n_groups * unroll, n_tiles - 1):
        scores(t + 1, (t + 1) % 2)
        update(t, t % 2)
    update(n_tiles - 1, (n_tiles - 1) % 2)


def _mla_attn_kernel(qtab, ktab, q_ref, k_ref, v_ref, o_ref, m_ref, acc_ref, s_buf, rmax_buf, mask_ref,
                     *, tq, nq):
    row = lax.broadcasted_iota(jnp.int32, (tq, tq), 0)
    col = lax.broadcasted_iota(jnp.int32, (tq, tq), 1)
    mask_ref[0] = jnp.where(row >= col, 0.0, NEG_BIG).astype(F32)
    mask_ref[1] = jnp.zeros((tq, tq), F32)
    streams = [(q_ref.at[0, hh], k_ref.at[0, hh], v_ref.at[0, hh], m_ref.at[hh], acc_ref.at[hh])
               for hh in range(2)]
    _causal_flash(streams, qtab, ktab, nq * (nq + 1) // 2, tq, mask_ref, s_buf, rmax_buf, MLA_FLASH_UNROLL)
    lane = lax.broadcasted_iota(jnp.int32, (1, LANES), 1)
    for qi in range(nq):
        outs = []
        for hh in range(2):
            acc = acc_ref[hh, qi]
            denom_lane = MLA_V if hh == 0 else 0
            denom = jnp.sum(jnp.where(lane == denom_lane, acc, 0.0), axis=1, keepdims=True)
            outs.append(acc / denom)
        o_ref[0, qi * tq:(qi + 1) * tq, :] = jnp.where(lane < MLA_V, outs[0], outs[1]).astype(BF16)


def _mla_attn(q, k, v, tq):
    b, nh, s, _ = q.shape
    nq = s // tq
    qtab, ktab = _tile_tables(nq)
    smem = pl.BlockSpec(memory_space=pltpu.SMEM)
    seq = pl.BlockSpec((1, 2, s, LANES), lambda bi, hp: (bi, hp, 0, 0))
    return pl.pallas_call(
        functools.partial(_mla_attn_kernel, tq=tq, nq=nq),
        grid=(b, nh // 2),
        in_specs=[smem, smem, seq, seq, seq],
        out_specs=pl.BlockSpec((1, s, LANES), lambda bi, hp: (bi, 0, hp)),
        out_shape=jax.ShapeDtypeStruct((b, s, nh * MLA_V), BF16),
        scratch_shapes=[pltpu.VMEM((2, nq, tq, LANES), F32), pltpu.VMEM((2, nq, tq, LANES), F32),
                        pltpu.VMEM((2, 2, tq, tq), F32), pltpu.VMEM((2, 2, tq, LANES), F32),
                        pltpu.VMEM((2, tq, tq), F32)],
        compiler_params=_cparams("parallel", "parallel"),
        name="mla_attn",
    )(qtab, ktab, q, k, v)


def _bucket_upper_bounds():
    n = np.arange(0, 4 * REL_MAX_DIST, dtype=np.int32)
    max_exact = REL_BUCKETS // 2
    nf = np.maximum(n, 1).astype(np.float32)
    large = max_exact + (np.log(nf / np.float32(max_exact)) / np.float32(math.log(REL_MAX_DIST / max_exact))
                         * np.float32(REL_BUCKETS - max_exact)).astype(np.int32)
    large = np.minimum(large, REL_BUCKETS - 1)
    bucket = np.where(n < max_exact, n, large)
    assert np.all(np.diff(bucket) >= 0) and bucket[-1] == REL_BUCKETS - 1
    return [int(np.max(n[bucket == b])) for b in range(REL_BUCKETS - 1)]


def _bias_tile_kernel(tbl_ref, o_ref, *, tq, bounds):
    h = pl.program_id(0)
    kind = pl.program_id(1)
    row = lax.broadcasted_iota(jnp.int32, (tq, tq), 0)
    col = lax.broadcasted_iota(jnp.int32, (tq, tq), 1)
    n = row - col + kind * tq
    far = tbl_ref[(REL_BUCKETS - 1) * DIFF_HEADS + h]
    r = jnp.zeros((tq, tq), F32)
    for bkt in range(REL_BUCKETS - 2, -1, -1):
        r = jnp.where(n <= bounds[bkt], (tbl_ref[bkt * DIFF_HEADS + h] - far) * LOG2E, r)
    o_ref[0, 0] = jnp.where(n >= 0, r, NEG_BIG)


def _bias_tiles(rel_bias, tq):
    bounds = _bucket_upper_bounds()
    assert bounds[-1] < tq, "bias must be constant from the second key block before the diagonal on"
    return pl.pallas_call(
        functools.partial(_bias_tile_kernel, tq=tq, bounds=bounds),
        grid=(DIFF_HEADS, 3),
        in_specs=[pl.BlockSpec(memory_space=pltpu.SMEM)],
        out_specs=pl.BlockSpec((1, 1, tq, tq), lambda h, kd: (h, kd, 0, 0)),
        out_shape=jax.ShapeDtypeStruct((DIFF_HEADS, 3, tq, tq), F32),
        compiler_params=_cparams("parallel", "parallel"),
        name="rel_bias_tiles",
    )(rel_bias.reshape(-1))


def _diff_attn_kernel(qtab, ktab, q_ref, k_ref, v_ref, bias_ref, lq1_ref, lk1_ref, lq2_ref, lk2_ref, gsub_ref,
                      o_ref, m_ref, acc_ref, s_buf, rmax_buf, qm_ref, vx_ref, *, tq, nq, lam_init):
    lam = (jnp.exp(jnp.sum(lq1_ref[...] * lk1_ref[...], axis=1, keepdims=True))
           - jnp.exp(jnp.sum(lq2_ref[...] * lk2_ref[...], axis=1, keepdims=True)) + lam_init)
    q = q_ref[0]
    lane = lax.broadcasted_iota(jnp.int32, (1, LANES), 1)
    zero = jnp.zeros_like(q)
    qm_ref[0] = jnp.where(lane < DIFF_HEAD, q, zero)
    qm_ref[1] = jnp.where(lane >= DIFF_HEAD, q, zero)
    vx_ref[:, :DIFF_V] = v_ref[0]
    vx_ref[:, DIFF_V:] = jnp.broadcast_to((lane == 0).astype(BF16), (vx_ref.shape[0], LANES))
    streams = [(qm_ref.at[mp], k_ref.at[0], vx_ref, m_ref.at[mp], acc_ref.at[mp]) for mp in range(2)]
    _causal_flash(streams, qtab, ktab, nq * (nq + 1) // 2, tq, bias_ref.at[0], s_buf, rmax_buf, DIFF_FLASH_UNROLL)
    for qi in range(nq):
        outs = []
        for mp in range(2):
            acc = acc_ref[mp, qi]
            outs.append(acc[:, :DIFF_V] / jnp.sum(acc[:, DIFF_V:], axis=1, keepdims=True))
        o = outs[0] - lam * outs[1]
        o_ref[0, qi * tq:(qi + 1) * tq, :] = (_rms(o, gsub_ref[...]) * (1.0 - lam_init)).astype(BF16)


def _diff_attn(l, dq, dk, dv, bias_tiles, layer_ops, lam_init, tq):
    b, s, _ = dq.shape
    nq = s // tq
    qtab, ktab = _tile_tables(nq)
    smem = pl.BlockSpec(memory_space=pltpu.SMEM)
    seq = pl.BlockSpec((1, s, LANES), lambda bi, h: (bi, 0, h))
    n_kinds = bias_tiles.shape[1]
    return pl.pallas_call(
        functools.partial(_diff_attn_kernel, tq=tq, nq=nq, lam_init=lam_init),
        grid=(b, DIFF_HEADS),
        in_specs=[smem, smem, seq, seq, seq,
                  pl.BlockSpec((1, n_kinds, tq, tq), lambda bi, h: (h, 0, 0, 0))]
                 + [_layer_block(a, l) for a in layer_ops],
        out_specs=seq,
        out_shape=jax.ShapeDtypeStruct((b, s, W_DV), BF16),
        scratch_shapes=[pltpu.VMEM((2, nq, tq, LANES), F32), pltpu.VMEM((2, nq, tq, 2 * LANES), F32),
                        pltpu.VMEM((2, 2, tq, tq), F32), pltpu.VMEM((2, 2, tq, LANES), F32),
                        pltpu.VMEM((2, s, LANES), BF16), pltpu.VMEM((s, 2 * LANES), BF16)],
        compiler_params=_cparams("parallel", "parallel"),
        name="diff_attn",
    )(qtab, ktab, dq, dk, dv, bias_tiles, *layer_ops)


def _conv_ln_silu(a, halo, w_ref, b_ref, lng_ref, lnb_ref, buf_ref, shift_ref, tm):
    buf_ref[0:CONV_HALO, :] = halo
    buf_ref[CONV_HALO:, :] = a
    n_rows = tm + CONV_HALO - SUBLANES
    for r in range(1, SUBLANES):
        shift_ref[r - 1] = buf_ref[r:r + n_rows, :]
    base = CONV_HALO - (CONV_WIDTH - 1)
    y = jnp.zeros((tm, CONV_DIM), F32) + b_ref[...]
    for j in range(CONV_WIDTH):
        grp, r = divmod(base + j, SUBLANES)
        lo = grp * SUBLANES
        tap = buf_ref[lo:lo + tm, :] if r == 0 else shift_ref[r - 1, lo:lo + tm, :]
        y = y + w_ref[j:j + 1, :] * tap
    mu = jnp.mean(y, axis=-1, keepdims=True)
    var = jnp.mean(jnp.square(y - mu), axis=-1, keepdims=True)
    z = (y - mu) * lax.rsqrt(var + EPS) * lng_ref[...] + lnb_ref[...]
    return (z * jax.nn.sigmoid(z)).astype(BF16)


def _merge_kernel(x_ref, g_ref, wg_ref, bg_ref, a_ref, halo_ref, cw_ref, cb_ref, lng_ref, lnb_ref,
                  yb_ref, yc_ref, wa_ref, wb_ref, wc_ref, wo_ref, o_ref, buf_ref, shift_ref, *, tm, tiles_per_seq):
    x = x_ref[...]
    h = _rms(x, g_ref[...]).astype(BF16)
    halo = halo_ref[...]
    starts_seq = (pl.program_id(0) % tiles_per_seq) == 0
    ya = _conv_ln_silu(a_ref[...], jnp.where(starts_seq, jnp.zeros_like(halo), halo), cw_ref, cb_ref, lng_ref,
                       lnb_ref, buf_ref, shift_ref, tm)
    merged = None
    for br, (y, w_ref) in enumerate(((ya, wa_ref), (yb_ref[...], wb_ref), (yc_ref[...], wc_ref))):
        sl = slice(br * D_MODEL, (br + 1) * D_MODEL)
        gate = jax.nn.sigmoid(_dot(h, wg_ref[:, sl]) + bg_ref[:, sl])
        term = gate * _dot(y, w_ref[...])
        merged = term if merged is None else merged + term
    o_ref[...] = x + _dot(merged.astype(BF16), wo_ref[...])


def _merge(l, x2, a2, yb, yc, layer_ops, tm, tiles_per_seq):
    t = x2.shape[0]
    per = tm // CONV_HALO
    row = lambda w: pl.BlockSpec((tm, w), lambda i: (i, 0))
    halo = pl.BlockSpec((CONV_HALO, CONV_DIM), lambda i: (jnp.maximum(i * per - 1, 0), 0))
    g, wg, bg, cw, cb, lng, lnb, wa, wb, wc, wo = layer_ops
    lb = lambda a: _layer_block(a, l)
    return pl.pallas_call(
        functools.partial(_merge_kernel, tm=tm, tiles_per_seq=tiles_per_seq),
        grid=(t // tm,),
        in_specs=[row(D_MODEL), lb(g), lb(wg), lb(bg), row(CONV_DIM), halo, lb(cw), lb(cb), lb(lng),
                  lb(lnb), row(MLA_HEADS * MLA_V), row(W_DV), lb(wa), lb(wb), lb(wc), lb(wo)],
        out_specs=row(D_MODEL),
        out_shape=jax.ShapeDtypeStruct((t, D_MODEL), F32),
        scratch_shapes=[pltpu.VMEM((tm + CONV_HALO, CONV_DIM), F32),
                        pltpu.VMEM((SUBLANES - 1, tm + CONV_HALO - SUBLANES, CONV_DIM), F32)],
        compiler_params=_cparams("parallel"),
        name="merge",
    )(x2, g, wg, bg, a2, a2, cw, cb, lng, lnb, yb, yc, wa, wb, wc, wo)


def _ffn_kernel(x_ref, g_ref, wv_ref, wg_ref, cw_ref, cb_ref, wd_ref, gf_ref, o_ref,
                h_ref, buf_ref, carry_ref, *, tm, tiles_per_seq, final_norm):
    i = pl.program_id(0)
    x = x_ref[...]
    h_ref[...] = _rms(x, g_ref[...]).astype(BF16)
    o_ref[...] = x
    pad = SUBLANES

    @pl.when((i % tiles_per_seq) == 0)
    def _():
        carry_ref[...] = jnp.zeros_like(carry_ref)

    def up(c, slot):
        h = h_ref[...]
        buf_ref[slot, 0, pad:, :] = _dot(h, wv_ref[c])
        buf_ref[slot, 1, pad:, :] = _dot(h, wg_ref[c])

    def conv3(c, slot, half):
        buf = buf_ref.at[slot, half]
        buf[0:pad, :] = carry_ref[c, half]
        carry_ref[c, half] = buf[tm:tm + pad, :]
        w = cw_ref[c, half]
        return (w[0:1, :] * buf[pad - 2:pad - 2 + tm, :] + w[1:2, :] * buf[pad - 1:pad - 1 + tm, :]
                + w[2:3, :] * buf[pad:pad + tm, :] + cb_ref[c, half])

    def down(c, slot):
        uv = conv3(c, slot, 0)
        ug = conv3(c, slot, 1)
        act = (uv * jax.nn.sigmoid(uv) * ug).astype(BF16)
        o_ref[...] += _dot(act, wd_ref[c])

    def pair(cc, carry):
        c = 2 * cc
        up(c + 1, 1)
        down(c, 0)
        up(c + 2, 0)
        down(c + 1, 1)
        return carry

    up(0, 0)
    n_pairs = (N_FF_CHUNK - 1) // 2
    lax.fori_loop(0, n_pairs, pair, 0)
    if (N_FF_CHUNK - 1) % 2 == 1:
        up(N_FF_CHUNK - 1, 1)
        down(N_FF_CHUNK - 2, 0)
        down(N_FF_CHUNK - 1, 1)
    else:
        down(N_FF_CHUNK - 1, 0)
    if final_norm:
        o_ref[...] = _rms(o_ref[...], gf_ref[...])


def _ffn(l, x2, layer_ops, gf, tm, tiles_per_seq, final_norm):
    t = x2.shape[0]
    row = pl.BlockSpec((tm, D_MODEL), lambda i: (i, 0))
    return pl.pallas_call(
        functools.partial(_ffn_kernel, tm=tm, tiles_per_seq=tiles_per_seq, final_norm=final_norm),
        grid=(t // tm,),
        in_specs=[row] + [_layer_block(a, l) for a in layer_ops]
                 + [pl.BlockSpec(gf.shape, lambda i: (0, 0), pipeline_mode=pl.Buffered(1))],
        out_specs=row,
        out_shape=jax.ShapeDtypeStruct((t, D_MODEL), F32),
        scratch_shapes=[pltpu.VMEM((tm, D_MODEL), BF16),
                        pltpu.VMEM((2, 2, tm + SUBLANES, FF_CHUNK), F32),
                        pltpu.VMEM((N_FF_CHUNK, 2, SUBLANES, FF_CHUNK), F32)],
        compiler_params=_cparams("arbitrary"),
        name="ffn",
    )(x2, *layer_ops, gf)


def _rot_cols(w):
    half = w.shape[-1] // 2
    return jnp.concatenate([-w[..., half:], w[..., :half]], axis=-1)


def _prep_in_proj_weight(w_in):
    cuts = np.cumsum([W_GLU, MLA_Q_RANK, MLA_KV_RANK, MLA_ROPE, W_DQK, W_DQK, W_DV])
    glu, cq, ckv, kr, dq, dk, dv, gates = jnp.split(w_in, cuts, axis=-1)
    z = lambda n: jnp.zeros(w_in.shape[:-1] + (n,), w_in.dtype)
    tail = LANES - MLA_NOPE - MLA_ROPE
    w = jnp.concatenate([glu, cq, ckv, z(MLA_NOPE), kr, z(tail), z(MLA_NOPE), _rot_cols(kr), z(tail), dq, dk, dv],
                        axis=-1)
    return w.astype(BF16), gates.astype(BF16)


def _prep_mla_weights(w_uq, w_ukv):
    dqk = MLA_NOPE + MLA_ROPE
    wq = w_uq.reshape(w_uq.shape[:-1] + (MLA_HEADS, dqk))
    pad = jnp.zeros(wq.shape[:-1] + (LANES - dqk,), wq.dtype)
    wqa = jnp.concatenate([wq, pad], axis=-1)
    wqb = jnp.concatenate([jnp.zeros_like(wq[..., :MLA_NOPE]), _rot_cols(wq[..., MLA_NOPE:]), pad], axis=-1)
    wkv = w_ukv.reshape(w_ukv.shape[:-1] + (MLA_HEADS, MLA_NOPE + MLA_V))
    zk = jnp.zeros(wkv.shape[:-1] + (LANES - MLA_NOPE,), wkv.dtype)
    wk = jnp.concatenate([wkv[..., :MLA_NOPE], zk], axis=-1)
    zv = jnp.zeros(wkv.shape[:-1] + (LANES - MLA_V,), wkv.dtype)
    v_even = jnp.concatenate([wkv[..., MLA_NOPE:], zv], axis=-1)
    v_odd = jnp.concatenate([zv, wkv[..., MLA_NOPE:]], axis=-1)
    odd = (jnp.arange(MLA_HEADS) % 2 == 1)[:, None]
    wv = jnp.where(odd, v_odd, v_even)
    flat = lambda a: a.reshape(a.shape[:-2] + (MLA_HEADS * LANES,)).astype(BF16)
    return flat(wqa), flat(wqb), flat(wk), flat(wv)


def _rope_tables(positions):
    half = MLA_ROPE // 2
    freqs = ROPE_THETA ** (-jnp.arange(half, dtype=F32) / half)
    ang = positions.astype(F32)[..., None] * freqs
    cos, sin = jnp.cos(ang), jnp.sin(ang)
    b, s = positions.shape
    ones = jnp.ones((b, s, MLA_NOPE), F32)
    zeros_n = jnp.zeros((b, s, MLA_NOPE), F32)
    zeros_t = jnp.zeros((b, s, LANES - MLA_NOPE - MLA_ROPE), F32)
    cos_t = jnp.concatenate([ones, cos, cos, zeros_t], axis=-1)
    sin_t = jnp.concatenate([zeros_n, sin, sin, zeros_t], axis=-1)
    return cos_t, sin_t


def _chunk_cols(w):
    return jnp.moveaxis(w.reshape(w.shape[:-1] + (-1, FF_CHUNK)), -2, -3)


def kernel(x, positions, rel_bias, norm_mix, w_in, gate_bias, conv_w, conv_b, conv_ln_g, conv_ln_b, w_conv_out,
           mla_q_norm, w_uq, mla_kv_norm, w_ukv, w_mla_out, diff_lam_q1, diff_lam_k1, diff_lam_q2, diff_lam_k2,
           diff_sub_norm, w_diff_out, w_out, norm_ffn, w_up, ffn_conv_w, ffn_conv_b, w_down, norm_final):
    b, s, d = x.shape
    depth = w_in.shape[0]
    assert d == D_MODEL
    tm = min(512, s)
    tm_ffn = min(1024, s)
    tq = min(512, s)
    assert s % tm == 0 and s % tm_ffn == 0 and s % tq == 0 and tm % CONV_HALO == 0
    t = b * s
    vec = lambda a: a.reshape(depth, 1, -1)
    bf = lambda a: a.astype(BF16)

    cos_t, sin_t = (tbl.reshape(t, LANES) for tbl in _rope_tables(positions))
    bias_tiles = _bias_tiles(rel_bias, tq)

    w_in_p, w_gate = _prep_in_proj_weight(w_in)
    wqa, wqb, wk, wv = _prep_mla_weights(w_uq, w_ukv)
    in_proj_ops = (vec(norm_mix), w_in_p, vec(mla_q_norm), vec(mla_kv_norm), wqa, wqb, wk, wv)
    diff_ops = (vec(diff_lam_q1), vec(diff_lam_k1), vec(diff_lam_q2), vec(diff_lam_k2), vec(diff_sub_norm))
    merge_ops = (vec(norm_mix), w_gate, vec(gate_bias), conv_w, vec(conv_b), vec(conv_ln_g), vec(conv_ln_b),
                 bf(w_conv_out), bf(w_mla_out), bf(w_diff_out), bf(w_out))
    w_up_b = bf(w_up)
    ffn_ops = (vec(norm_ffn), _chunk_cols(w_up_b[..., :D_FF]), _chunk_cols(w_up_b[..., D_FF:]),
               ffn_conv_w.reshape(depth, FFN_CONV, 2, N_FF_CHUNK, FF_CHUNK).transpose(0, 3, 2, 1, 4),
               ffn_conv_b.reshape(depth, 2, N_FF_CHUNK, 1, FF_CHUNK).transpose(0, 2, 1, 3, 4),
               bf(w_down).reshape(depth, N_FF_CHUNK, FF_CHUNK, D_MODEL))

    x2 = x.reshape(t, d)
    for l in range(depth):
        a, dq, dk, dv, q, k, v = _in_proj(l, x2, cos_t, sin_t, in_proj_ops, tm, b, s)
        y_b = _mla_attn(q, k, v, tq)
        lam_init = 0.8 - 0.6 * math.exp(-0.3 * l)
        y_c = _diff_attn(l, dq.reshape(b, s, W_DQK), dk.reshape(b, s, W_DQK), dv.reshape(b, s, W_DV), bias_tiles,
                         diff_ops, lam_init, tq)
        x2 = _merge(l, x2, a, y_b.reshape(t, MLA_HEADS * MLA_V), y_c.reshape(t, W_DV), merge_ops, tm, s // tm)
        x2 = _ffn(l, x2, ffn_ops, norm_final.reshape(1, -1), tm_ffn, s // tm_ffn, l == depth - 1)
    return x2.reshape(b, s, d)
```

```python
import functools
import math

import numpy as np
import jax
import jax.numpy as jnp
from jax import lax
from jax.experimental import pallas as pl
from jax.experimental.pallas import tpu as pltpu

F32 = jnp.float32
BF16 = jnp.bfloat16

D_MODEL = 1024
CONV_DIM = 512
CONV_WIDTH = 31
MLA_HEADS = 8
MLA_NOPE = 64
MLA_ROPE = 32
MLA_V = 64
MLA_Q_RANK = 256
MLA_KV_RANK = 128
ROPE_THETA = 10000.0
DIFF_HEADS = 4
DIFF_HEAD = 64
DIFF_V = 2 * DIFF_HEAD
REL_BUCKETS = 32
REL_MAX_DIST = 128
D_FF = 2816
FFN_CONV = 3
N_BRANCH = 3
EPS = 1e-6

LANES = 128
SUBLANES = 8
VMEM_LIMIT = 56 * 1024 * 1024
NEG_BIG = -1e30
LOG2E = math.log2(math.e)

W_GLU = 2 * CONV_DIM
W_DQK = DIFF_HEADS * 2 * DIFF_HEAD
W_DV = DIFF_HEADS * DIFF_V
W_GATE = N_BRANCH * D_MODEL
W_SMALL = MLA_Q_RANK + MLA_KV_RANK + 2 * LANES
FF_CHUNK = 256
N_FF_CHUNK = D_FF // FF_CHUNK
CONV_HALO = 32
MLA_FLASH_UNROLL = 4
DIFF_FLASH_UNROLL = 2


def _cparams(*sem):
    return pltpu.CompilerParams(dimension_semantics=sem, vmem_limit_bytes=VMEM_LIMIT)


def _rms(x, g):
    return x * lax.rsqrt(jnp.mean(x * x, axis=-1, keepdims=True) + EPS) * g


def _dot(a, b):
    return jnp.dot(a, b, preferred_element_type=F32)


def _dot_nt(a, b):
    return lax.dot_general(a, b, (((1,), (1,)), ((), ())), preferred_element_type=F32)


def _layer_block(stacked, l):
    zeros = (0,) * (stacked.ndim - 1)
    return pl.BlockSpec((None,) + stacked.shape[1:], lambda *_: (l,) + zeros, pipeline_mode=pl.Buffered(1))


def _mla_heads(sm, cos, sin, gq_ref, gkv_ref, wqa_ref, wqb_ref, wk_ref, wv_ref, q_ref, k_ref, v_ref):
    cq = sm[:, :MLA_Q_RANK]
    ckv = sm[:, MLA_Q_RANK:MLA_Q_RANK + MLA_KV_RANK]
    kr = sm[:, MLA_Q_RANK + MLA_KV_RANK:MLA_Q_RANK + MLA_KV_RANK + LANES]
    kr_rot = sm[:, MLA_Q_RANK + MLA_KV_RANK + LANES:]
    hq = _rms(cq, gq_ref[...]).astype(BF16)
    hkv = _rms(ckv, gkv_ref[...]).astype(BF16)
    qa = _dot(hq, wqa_ref[...])
    qb = _dot(hq, wqb_ref[...])
    kn = _dot(hkv, wk_ref[...])
    vv = _dot(hkv, wv_ref[...])
    kpe = kr * cos + kr_rot * sin
    lane = lax.broadcasted_iota(jnp.int32, (1, LANES), 1)
    scale = (MLA_NOPE + MLA_ROPE) ** -0.5 * LOG2E
    for h in range(MLA_HEADS):
        sl = slice(h * LANES, (h + 1) * LANES)
        q_ref[0, h] = ((qa[:, sl] * cos + qb[:, sl] * sin) * scale).astype(BF16)
        k_ref[0, h] = (kn[:, sl] + kpe).astype(BF16)
        ones = (lane >= MLA_V) if h % 2 == 0 else (lane < MLA_V)
        v_ref[0, h] = (vv[:, sl] + ones.astype(F32)).astype(BF16)


def _in_proj_kernel(x_ref, g_ref, w_ref, cos_ref, sin_ref, gq_ref, gkv_ref, wqa_ref, wqb_ref, wk_ref, wv_ref,
                    a_ref, dq_ref, dk_ref, dv_ref, q_ref, k_ref, v_ref):
    h = _rms(x_ref[...], g_ref[...]).astype(BF16)
    c0 = 0
    u = _dot(h, w_ref[:, c0:c0 + W_GLU])
    a_ref[...] = u[:, :CONV_DIM] * jax.nn.sigmoid(u[:, CONV_DIM:])
    c0 += W_GLU
    _mla_heads(_dot(h, w_ref[:, c0:c0 + W_SMALL]), cos_ref[...], sin_ref[...], gq_ref, gkv_ref,
               wqa_ref, wqb_ref, wk_ref, wv_ref, q_ref, k_ref, v_ref)
    c0 += W_SMALL
    dq_ref[...] = (_dot(h, w_ref[:, c0:c0 + W_DQK]) * (DIFF_HEAD ** -0.5 * LOG2E)).astype(BF16)
    c0 += W_DQK
    dk_ref[...] = _dot(h, w_ref[:, c0:c0 + W_DQK]).astype(BF16)
    c0 += W_DQK
    dv_ref[...] = _dot(h, w_ref[:, c0:c0 + W_DV]).astype(BF16)


def _in_proj(l, x2, cos_t, sin_t, layer_ops, tm, b, s):
    t = x2.shape[0]
    tiles_per_seq = s // tm
    row = lambda n: pl.BlockSpec((tm, n), lambda i: (i, 0))
    heads = pl.BlockSpec((1, MLA_HEADS, tm, LANES), lambda i: (i // tiles_per_seq, 0, i % tiles_per_seq, 0))
    g, w, gq, gkv, wqa, wqb, wk, wv = layer_ops
    lb = lambda a: _layer_block(a, l)
    return pl.pallas_call(
        _in_proj_kernel,
        grid=(t // tm,),
        in_specs=[row(D_MODEL), lb(g), lb(w), row(LANES), row(LANES), lb(gq), lb(gkv),
                  lb(wqa), lb(wqb), lb(wk), lb(wv)],
        out_specs=[row(CONV_DIM), row(W_DQK), row(W_DQK), row(W_DV), heads, heads, heads],
        out_shape=[jax.ShapeDtypeStruct((t, CONV_DIM), F32),
                   jax.ShapeDtypeStruct((t, W_DQK), BF16),
                   jax.ShapeDtypeStruct((t, W_DQK), BF16),
                   jax.ShapeDtypeStruct((t, W_DV), BF16)]
                  + [jax.ShapeDtypeStruct((b, MLA_HEADS, s, LANES), BF16)] * 3,
        compiler_params=_cparams("parallel"),
        name="in_proj",
    )(x2, g, w, cos_t, sin_t, gq, gkv, wqa, wqb, wk, wv)


def _score_tile(q, k, bias, s_ref, rmax_ref):
    s = _dot_nt(q, k) + bias
    s_ref[...] = s
    chunks = [s[:, c:c + LANES] for c in range(0, s.shape[1], LANES)]
    row_max = jnp.max(functools.reduce(jnp.maximum, chunks), axis=1, keepdims=True)
    rmax_ref[...] = jnp.broadcast_to(row_max, rmax_ref.shape)


def _softmax_update(s_ref, rmax_ref, v, m_ref, acc_ref):
    m_old = m_ref[...]
    m_new = jnp.maximum(m_old, rmax_ref[...])
    alpha = jnp.exp2(m_old - m_new)
    p = jnp.concatenate([jnp.exp2(s_ref[:, c:c + LANES] - m_new) for c in range(0, s_ref.shape[1], LANES)], axis=1)
    pv = _dot(p.astype(BF16), v)
    acc_ref[...] = jnp.concatenate([alpha] * (acc_ref.shape[1] // LANES), axis=1) * acc_ref[...] + pv
    m_ref[...] = m_new


def _causal_tiles(nq):
    return [(qi, kj) for qi in range(nq) for kj in range(qi + 1)]


def _tile_tables(nq):
    tiles = np.asarray(_causal_tiles(nq), dtype=np.int32)
    return jnp.asarray(tiles[:, 0]), jnp.asarray(tiles[:, 1])


def _causal_flash(streams, qtab, ktab, n_tiles, tq, bias_ref, s_buf, rmax_buf, unroll):
    n_kinds = bias_ref.shape[0]

    def blk(ref, j):
        return ref[pl.ds(pl.multiple_of(j * tq, tq), tq), :]

    def scores(t, dst):
        qi, kj = qtab[t], ktab[t]
        bias = bias_ref[jnp.minimum(qi - kj, n_kinds - 1)]
        for si, (q_ref, k_ref, _, _, _) in enumerate(streams):
            _score_tile(blk(q_ref, qi), blk(k_ref, kj), bias, s_buf.at[dst, si], rmax_buf.at[dst, si])

    def update(t, src):
        qi, kj = qtab[t], ktab[t]
        for si, (_, _, v_ref, m_ref, acc_ref) in enumerate(streams):
            _softmax_update(s_buf.at[src, si], rmax_buf.at[src, si], blk(v_ref, kj), m_ref.at[qi], acc_ref.at[qi])

    def group(jj, carry):
        for u in range(unroll):
            t = unroll * jj + u
            scores(t + 1, (u + 1) % 2)
            update(t, u % 2)
        return carry

    for _, _, _, m_ref, acc_ref in streams:
        m_ref[...] = jnp.full(m_ref.shape, NEG_BIG, F32)
        acc_ref[...] = jnp.zeros(acc_ref.shape, F32)
    scores(0, 0)
    assert unroll % 2 == 0
    n_groups = (n_tiles - 1) // unroll
    lax.fori_loop(0, n_groups, group, 0)
    for t in range(n_groups * unroll, n_tiles - 1):
        scores(t + 1, (t + 1) % 2)
        update(t, t % 2)
    update(n_tiles - 1, (n_tiles - 1) % 2)


def _mla_attn_kernel(qtab, ktab, q_ref, k_ref, v_ref, o_ref, m_ref, acc_ref, s_buf, rmax_buf, mask_ref,
                     *, tq, nq):
    @pl.when((pl.program_id(0) == 0) & (pl.program_id(1) == 0))
    def _():
        row = lax.broadcasted_iota(jnp.int32, (tq, tq), 0)
        col = lax.broadcasted_iota(jnp.int32, (tq, tq), 1)
        mask_ref[0] = jnp.where(row >= col, 0.0, NEG_BIG).astype(F32)
        mask_ref[1] = jnp.zeros((tq, tq), F32)

    streams = [(q_ref.at[0, hh], k_ref.at[0, hh], v_ref.at[0, hh], m_ref.at[hh], acc_ref.at[hh])
               for hh in range(2)]
    _causal_flash(streams, qtab, ktab, nq * (nq + 1) // 2, tq, mask_ref, s_buf, rmax_buf, MLA_FLASH_UNROLL)
    low = lax.broadcasted_iota(jnp.int32, (1, LANES), 1) < MLA_V
    for qi in range(nq):
        even, odd = acc_ref[0, qi], acc_ref[1, qi]
        num = jnp.where(low, even, odd)
        den = pltpu.roll(jnp.where(low, odd, even), MLA_V, axis=1)
        o_ref[0, qi * tq:(qi + 1) * tq, :] = (num / den).astype(BF16)


def _mla_attn(q, k, v, tq):
    b, nh, s, _ = q.shape
    nq = s // tq
    qtab, ktab = _tile_tables(nq)
    smem = pl.BlockSpec(memory_space=pltpu.SMEM)
    seq = pl.BlockSpec((1, 2, s, LANES), lambda bi, hp: (bi, hp, 0, 0))
    return pl.pallas_call(
        functools.partial(_mla_attn_kernel, tq=tq, nq=nq),
        grid=(b, nh // 2),
        in_specs=[smem, smem, seq, seq, seq],
        out_specs=pl.BlockSpec((1, s, LANES), lambda bi, hp: (bi, 0, hp)),
        out_shape=jax.ShapeDtypeStruct((b, s, nh * MLA_V), BF16),
        scratch_shapes=[pltpu.VMEM((2, nq, tq, LANES), F32), pltpu.VMEM((2, nq, tq, LANES), F32),
                        pltpu.VMEM((2, 2, tq, tq), F32), pltpu.VMEM((2, 2, tq, LANES), F32),
                        pltpu.VMEM((2, tq, tq), F32)],
        compiler_params=_cparams("arbitrary", "arbitrary"),
        name="mla_attn",
    )(qtab, ktab, q, k, v)


def _bucket_upper_bounds():
    n = np.arange(0, 4 * REL_MAX_DIST, dtype=np.int32)
    max_exact = REL_BUCKETS // 2
    nf = np.maximum(n, 1).astype(np.float32)
    large = max_exact + (np.log(nf / np.float32(max_exact)) / np.float32(math.log(REL_MAX_DIST / max_exact))
                         * np.float32(REL_BUCKETS - max_exact)).astype(np.int32)
    large = np.minimum(large, REL_BUCKETS - 1)
    bucket = np.where(n < max_exact, n, large)
    assert np.all(np.diff(bucket) >= 0) and bucket[-1] == REL_BUCKETS - 1
    return [int(np.max(n[bucket == b])) for b in range(REL_BUCKETS - 1)]


def _bias_tile_kernel(tbl_ref, o_ref, *, tq, bounds):
    h = pl.program_id(0)
    kind = pl.program_id(1)
    row = lax.broadcasted_iota(jnp.int32, (tq, tq), 0)
    col = lax.broadcasted_iota(jnp.int32, (tq, tq), 1)
    n = row - col + kind * tq
    far = tbl_ref[(REL_BUCKETS - 1) * DIFF_HEADS + h]
    r = jnp.zeros((tq, tq), F32)
    for bkt in range(REL_BUCKETS - 2, -1, -1):
        r = jnp.where(n <= bounds[bkt], (tbl_ref[bkt * DIFF_HEADS + h] - far) * LOG2E, r)
    o_ref[0, 0] = jnp.where(n >= 0, r, NEG_BIG)


def _bias_tiles(rel_bias, tq):
    bounds = _bucket_upper_bounds()
    assert bounds[-1] < tq, "bias must be constant from the second key block before the diagonal on"
    return pl.pallas_call(
        functools.partial(_bias_tile_kernel, tq=tq, bounds=bounds),
        grid=(DIFF_HEADS, 3),
        in_specs=[pl.BlockSpec(memory_space=pltpu.SMEM)],
        out_specs=pl.BlockSpec((1, 1, tq, tq), lambda h, kd: (h, kd, 0, 0)),
        out_shape=jax.ShapeDtypeStruct((DIFF_HEADS, 3, tq, tq), F32),
        compiler_params=_cparams("parallel", "parallel"),
        name="rel_bias_tiles",
    )(rel_bias.reshape(-1))


def _diff_attn_kernel(qtab, ktab, q_ref, k_ref, v_ref, bias_ref, lq1_ref, lk1_ref, lq2_ref, lk2_ref, gsub_ref,
                      o_ref, m_ref, acc_ref, s_buf, rmax_buf, qm_ref, vx_ref, *, tq, nq, lam_init):
    lam = (jnp.exp(jnp.sum(lq1_ref[...] * lk1_ref[...], axis=1, keepdims=True))
           - jnp.exp(jnp.sum(lq2_ref[...] * lk2_ref[...], axis=1, keepdims=True)) + lam_init)
    q = q_ref[0]
    lane = lax.broadcasted_iota(jnp.int32, (1, LANES), 1)
    zero = jnp.zeros_like(q)
    qm_ref[0] = jnp.where(lane < DIFF_HEAD, q, zero)
    qm_ref[1] = jnp.where(lane >= DIFF_HEAD, q, zero)
    vx_ref[:, :DIFF_V] = v_ref[0]
    vx_ref[:, DIFF_V:] = jnp.ones((vx_ref.shape[0], LANES), BF16)
    streams = [(qm_ref.at[mp], k_ref.at[0], vx_ref, m_ref.at[mp], acc_ref.at[mp]) for mp in range(2)]
    _causal_flash(streams, qtab, ktab, nq * (nq + 1) // 2, tq, bias_ref.at[0], s_buf, rmax_buf, DIFF_FLASH_UNROLL)
    for qi in range(nq):
        outs = []
        for mp in range(2):
            acc = acc_ref[mp, qi]
            outs.append(acc[:, :DIFF_V] / acc[:, DIFF_V:])
        o = outs[0] - lam * outs[1]
        o_ref[0, qi * tq:(qi + 1) * tq, :] = (_rms(o, gsub_ref[...]) * (1.0 - lam_init)).astype(BF16)


def _diff_attn(l, dq, dk, dv, bias_tiles, layer_ops, lam_init, tq):
    b, s, _ = dq.shape
    nq = s // tq
    qtab, ktab = _tile_tables(nq)
    smem = pl.BlockSpec(memory_space=pltpu.SMEM)
    seq = pl.BlockSpec((1, s, LANES), lambda bi, h: (bi, 0, h))
    n_kinds = bias_tiles.shape[1]
    return pl.pallas_call(
        functools.partial(_diff_attn_kernel, tq=tq, nq=nq, lam_init=lam_init),
        grid=(b, DIFF_HEADS),
        in_specs=[smem, smem, seq, seq, seq,
                  pl.BlockSpec((1, n_kinds, tq, tq), lambda bi, h: (h, 0, 0, 0))]
                 + [_layer_block(a, l) for a in layer_ops],
        out_specs=seq,
        out_shape=jax.ShapeDtypeStruct((b, s, W_DV), BF16),
        scratch_shapes=[pltpu.VMEM((2, nq, tq, LANES), F32), pltpu.VMEM((2, nq, tq, 2 * LANES), F32),
                        pltpu.VMEM((2, 2, tq, tq), F32), pltpu.VMEM((2, 2, tq, LANES), F32),
                        pltpu.VMEM((2, s, LANES), BF16), pltpu.VMEM((s, 2 * LANES), BF16)],
        compiler_params=_cparams("parallel", "parallel"),
        name="diff_attn",
    )(qtab, ktab, dq, dk, dv, bias_tiles, *layer_ops)


def _conv_ln_silu(a, halo, w_ref, b_ref, lng_ref, lnb_ref, buf_ref, shift_ref, tm):
    buf_ref[0:CONV_HALO, :] = halo
    buf_ref[CONV_HALO:, :] = a
    n_rows = tm + CONV_HALO - SUBLANES
    for r in range(1, SUBLANES):
        shift_ref[r - 1] = buf_ref[r:r + n_rows, :]
    base = CONV_HALO - (CONV_WIDTH - 1)
    y = jnp.zeros((tm, CONV_DIM), F32) + b_ref[...]
    for j in range(CONV_WIDTH):
        grp, r = divmod(base + j, SUBLANES)
        lo = grp * SUBLANES
        tap = buf_ref[lo:lo + tm, :] if r == 0 else shift_ref[r - 1, lo:lo + tm, :]
        y = y + w_ref[j:j + 1, :] * tap
    mu = jnp.mean(y, axis=-1, keepdims=True)
    var = jnp.mean(jnp.square(y - mu), axis=-1, keepdims=True)
    z = (y - mu) * lax.rsqrt(var + EPS) * lng_ref[...] + lnb_ref[...]
    return (z * jax.nn.sigmoid(z)).astype(BF16)


def _merge_kernel(x_ref, g_ref, wg_ref, bg_ref, a_ref, halo_ref, cw_ref, cb_ref, lng_ref, lnb_ref,
                  yb_ref, yc_ref, wa_ref, wb_ref, wc_ref, wo_ref, o_ref, buf_ref, shift_ref, *, tm, tiles_per_seq):
    x = x_ref[...]
    h = _rms(x, g_ref[...]).astype(BF16)
    halo = halo_ref[...]
    starts_seq = (pl.program_id(0) % tiles_per_seq) == 0
    ya = _conv_ln_silu(a_ref[...], jnp.where(starts_seq, jnp.zeros_like(halo), halo), cw_ref, cb_ref, lng_ref,
                       lnb_ref, buf_ref, shift_ref, tm)
    merged = None
    for br, (y, w_ref) in enumerate(((ya, wa_ref), (yb_ref[...], wb_ref), (yc_ref[...], wc_ref))):
        sl = slice(br * D_MODEL, (br + 1) * D_MODEL)
        gate = jax.nn.sigmoid(_dot(h, wg_ref[:, sl]) + bg_ref[:, sl])
        term = gate * _dot(y, w_ref[...])
        merged = term if merged is None else merged + term
    o_ref[...] = x + _dot(merged.astype(BF16), wo_ref[...])


def _merge(l, x2, a2, yb, yc, layer_ops, tm, tiles_per_seq):
    t = x2.shape[0]
    per = tm // CONV_HALO
    row = lambda w: pl.BlockSpec((tm, w), lambda i: (i, 0))
    halo = pl.BlockSpec((CONV_HALO, CONV_DIM), lambda i: (jnp.maximum(i * per - 1, 0), 0))
    g, wg, bg, cw, cb, lng, lnb, wa, wb, wc, wo = layer_ops
    lb = lambda a: _layer_block(a, l)
    return pl.pallas_call(
        functools.partial(_merge_kernel, tm=tm, tiles_per_seq=tiles_per_seq),
        grid=(t // tm,),
        in_specs=[row(D_MODEL), lb(g), lb(wg), lb(bg), row(CONV_DIM), halo, lb(cw), lb(cb), lb(lng),
                  lb(lnb), row(MLA_HEADS * MLA_V), row(W_DV), lb(wa), lb(wb), lb(wc), lb(wo)],
        out_specs=row(D_MODEL),
        out_shape=jax.ShapeDtypeStruct((t, D_MODEL), F32),
        scratch_shapes=[pltpu.VMEM((tm + CONV_HALO, CONV_DIM), F32),
                        pltpu.VMEM((SUBLANES - 1, tm + CONV_HALO - SUBLANES, CONV_DIM), F32)],
        compiler_params=_cparams("parallel"),
        name="merge",
    )(x2, g, wg, bg, a2, a2, cw, cb, lng, lnb, yb, yc, wa, wb, wc, wo)


def _ffn_kernel(x_ref, g_ref, wv_ref, wg_ref, cw_ref, cb_ref, wd_ref, gf_ref, o_ref,
                h_ref, buf_ref, carry_ref, *, tm, tiles_per_seq, final_norm):
    i = pl.program_id(0)
    x = x_ref[...]
    h_ref[...] = _rms(x, g_ref[...]).astype(BF16)
    o_ref[...] = x
    pad = SUBLANES

    @pl.when((i % tiles_per_seq) == 0)
    def _():
        carry_ref[...] = jnp.zeros_like(carry_ref)

    def up(c, slot):
        h = h_ref[...]
        buf_ref[slot, 0, pad:, :] = _dot(h, wv_ref[c])
        buf_ref[slot, 1, pad:, :] = _dot(h, wg_ref[c])

    def conv3(c, slot, half):
        buf = buf_ref.at[slot, half]
        buf[0:pad, :] = carry_ref[c, half]
        carry_ref[c, half] = buf[tm:tm + pad, :]
        w = cw_ref[c, half]
        return (w[0:1, :] * buf[pad - 2:pad - 2 + tm, :] + w[1:2, :] * buf[pad - 1:pad - 1 + tm, :]
                + w[2:3, :] * buf[pad:pad + tm, :] + cb_ref[c, half])

    def down(c, slot):
        uv = conv3(c, slot, 0)
        ug = conv3(c, slot, 1)
        act = (uv * jax.nn.sigmoid(uv) * ug).astype(BF16)
        o_ref[...] += _dot(act, wd_ref[c])

    def pair(cc, carry):
        c = 2 * cc
        up(c + 1, 1)
        down(c, 0)
        up(c + 2, 0)
        down(c + 1, 1)
        return carry

    up(0, 0)
    n_pairs = (N_FF_CHUNK - 1) // 2
    lax.fori_loop(0, n_pairs, pair, 0)
    if (N_FF_CHUNK - 1) % 2 == 1:
        up(N_FF_CHUNK - 1, 1)
        down(N_FF_CHUNK - 2, 0)
        down(N_FF_CHUNK - 1, 1)
    else:
        down(N_FF_CHUNK - 1, 0)
    if final_norm:
        o_ref[...] = _rms(o_ref[...], gf_ref[...])


def _ffn(l, x2, layer_ops, gf, tm, tiles_per_seq, final_norm):
    t = x2.shape[0]
    row = pl.BlockSpec((tm, D_MODEL), lambda i: (i, 0))
    return pl.pallas_call(
        functools.partial(_ffn_kernel, tm=tm, tiles_per_seq=tiles_per_seq, final_norm=final_norm),
        grid=(t // tm,),
        in_specs=[row] + [_layer_block(a, l) for a in layer_ops]
                 + [pl.BlockSpec(gf.shape, lambda i: (0, 0), pipeline_mode=pl.Buffered(1))],
        out_specs=row,
        out_shape=jax.ShapeDtypeStruct((t, D_MODEL), F32),
        scratch_shapes=[pltpu.VMEM((tm, D_MODEL), BF16),
                        pltpu.VMEM((2, 2, tm + SUBLANES, FF_CHUNK), F32),
                        pltpu.VMEM((N_FF_CHUNK, 2, SUBLANES, FF_CHUNK), F32)],
        compiler_params=_cparams("arbitrary"),
        name="ffn",
    )(x2, *layer_ops, gf)


def _rot_cols(w):
    half = w.shape[-1] // 2
    return jnp.concatenate([-w[..., half:], w[..., :half]], axis=-1)


def _prep_in_proj_weight(w_in):
    cuts = np.cumsum([W_GLU, MLA_Q_RANK, MLA_KV_RANK, MLA_ROPE, W_DQK, W_DQK, W_DV])
    glu, cq, ckv, kr, dq, dk, dv, gates = jnp.split(w_in, cuts, axis=-1)
    z = lambda n: jnp.zeros(w_in.shape[:-1] + (n,), w_in.dtype)
    tail = LANES - MLA_NOPE - MLA_ROPE
    w = jnp.concatenate([glu, cq, ckv, z(MLA_NOPE), kr, z(tail), z(MLA_NOPE), _rot_cols(kr), z(tail), dq, dk, dv],
                        axis=-1)
    return w.astype(BF16), gates.astype(BF16)


def _prep_mla_weights(w_uq, w_ukv):
    dqk = MLA_NOPE + MLA_ROPE
    wq = w_uq.reshape(w_uq.shape[:-1] + (MLA_HEADS, dqk))
    pad = jnp.zeros(wq.shape[:-1] + (LANES - dqk,), wq.dtype)
    wqa = jnp.concatenate([wq, pad], axis=-1)
    wqb = jnp.concatenate([jnp.zeros_like(wq[..., :MLA_NOPE]), _rot_cols(wq[..., MLA_NOPE:]), pad], axis=-1)
    wkv = w_ukv.reshape(w_ukv.shape[:-1] + (MLA_HEADS, MLA_NOPE + MLA_V))
    zk = jnp.zeros(wkv.shape[:-1] + (LANES - MLA_NOPE,), wkv.dtype)
    wk = jnp.concatenate([wkv[..., :MLA_NOPE], zk], axis=-1)
    zv = jnp.zeros(wkv.shape[:-1] + (LANES - MLA_V,), wkv.dtype)
    v_even = jnp.concatenate([wkv[..., MLA_NOPE:], zv], axis=-1)
    v_odd = jnp.concatenate([zv, wkv[..., MLA_NOPE:]], axis=-1)
    odd = (jnp.arange(MLA_HEADS) % 2 == 1)[:, None]
    wv = jnp.where(odd, v_odd, v_even)
    flat = lambda a: a.reshape(a.shape[:-2] + (MLA_HEADS * LANES,)).astype(BF16)
    return flat(wqa), flat(wqb), flat(wk), flat(wv)


def _rope_tables(positions):
    half = MLA_ROPE // 2
    freqs = ROPE_THETA ** (-jnp.arange(half, dtype=F32) / half)
    b, s = positions.shape
    ang = freqs[:, None] * positions.astype(F32).reshape(1, b * s)
    cos, sin = (jnp.swapaxes(fn(ang), 0, 1).reshape(b, s, half) for fn in (jnp.cos, jnp.sin))
    ones = jnp.ones((b, s, MLA_NOPE), F32)
    zeros_n = jnp.zeros((b, s, MLA_NOPE), F32)
    zeros_t = jnp.zeros((b, s, LANES - MLA_NOPE - MLA_ROPE), F32)
    cos_t = jnp.concatenate([ones, cos, cos, zeros_t], axis=-1)
    sin_t = jnp.concatenate([zeros_n, sin, sin, zeros_t], axis=-1)
    return cos_t, sin_t


def _chunk_cols(w):
    return jnp.moveaxis(w.reshape(w.shape[:-1] + (-1, FF_CHUNK)), -2, -3)


def kernel(x, positions, rel_bias, norm_mix, w_in, gate_bias, conv_w, conv_b, conv_ln_g, conv_ln_b, w_conv_out,
           mla_q_norm, w_uq, mla_kv_norm, w_ukv, w_mla_out, diff_lam_q1, diff_lam_k1, diff_lam_q2, diff_lam_k2,
           diff_sub_norm, w_diff_out, w_out, norm_ffn, w_up, ffn_conv_w, ffn_conv_b, w_down, norm_final):
    b, s, d = x.shape
    depth = w_in.shape[0]
    assert d == D_MODEL
    tm = min(512, s)
    tm_ffn = min(1024, s)
    tq = min(512, s)
    assert s % tm == 0 and s % tm_ffn == 0 and s % tq == 0 and tm % CONV_HALO == 0
    t = b * s
    vec = lambda a: a.reshape(depth, 1, -1)
    bf = lambda a: a.astype(BF16)

    cos_t, sin_t = (tbl.reshape(t, LANES) for tbl in _rope_tables(positions))
    bias_tiles = _bias_tiles(rel_bias, tq)

    w_in_p, w_gate = _prep_in_proj_weight(bf(w_in))
    wqa, wqb, wk, wv = _prep_mla_weights(bf(w_uq), bf(w_ukv))
    in_proj_ops = (vec(norm_mix), w_in_p, vec(mla_q_norm), vec(mla_kv_norm), wqa, wqb, wk, wv)
    diff_ops = (vec(diff_lam_q1), vec(diff_lam_k1), vec(diff_lam_q2), vec(diff_lam_k2), vec(diff_sub_norm))
    merge_ops = (vec(norm_mix), w_gate, vec(gate_bias), conv_w, vec(conv_b), vec(conv_ln_g), vec(conv_ln_b),
                 bf(w_conv_out), bf(w_mla_out), bf(w_diff_out), bf(w_out))
    w_up_b = bf(w_up)
    ffn_ops = (vec(norm_ffn), _chunk_cols(w_up_b[..., :D_FF]), _chunk_cols(w_up_b[..., D_FF:]),
               ffn_conv_w.reshape(depth, FFN_CONV, 2, N_FF_CHUNK, FF_CHUNK).transpose(0, 3, 2, 1, 4),
               ffn_conv_b.reshape(depth, 2, N_FF_CHUNK, 1, FF_CHUNK).transpose(0, 2, 1, 3, 4),
               bf(w_down).reshape(depth, N_FF_CHUNK, FF_CHUNK, D_MODEL))

    x2 = x.reshape(t, d)
    for l in range(depth):
        a, dq, dk, dv, q, k, v = _in_proj(l, x2, cos_t, sin_t, in_proj_ops, tm, b, s)
        y_b = _mla_attn(q, k, v, tq)
        lam_init = 0.8 - 0.6 * math.exp(-0.3 * l)
        y_c = _diff_attn(l, dq.reshape(b, s, W_DQK), dk.reshape(b, s, W_DQK), dv.reshape(b, s, W_DV), bias_tiles,
                         diff_ops, lam_init, tq)
        x2 = _merge(l, x2, a, y_b.reshape(t, MLA_HEADS * MLA_V), y_c.reshape(t, W_DV), merge_ops, tm, s // tm)
        x2 = _ffn(l, x2, ffn_ops, norm_final.reshape(1, -1), tm_ffn, s // tm_ffn, l == depth - 1)
    return x2.reshape(b, s, d)
```

```python
import functools
import math

import numpy as np
import jax
import jax.numpy as jnp
from jax import lax
from jax.experimental import pallas as pl
from jax.experimental.pallas import tpu as pltpu

F32 = jnp.float32
BF16 = jnp.bfloat16

D_MODEL = 1024
CONV_DIM = 512
CONV_WIDTH = 31
MLA_HEADS = 8
MLA_NOPE = 64
MLA_ROPE = 32
MLA_V = 64
MLA_Q_RANK = 256
MLA_KV_RANK = 128
ROPE_THETA = 10000.0
DIFF_HEADS = 4
DIFF_HEAD = 64
DIFF_V = 2 * DIFF_HEAD
REL_BUCKETS = 32
REL_MAX_DIST = 128
D_FF = 2816
FFN_CONV = 3
N_BRANCH = 3
EPS = 1e-6

LANES = 128
SUBLANES = 8
VMEM_LIMIT = 56 * 1024 * 1024
NEG_BIG = -1e30
LOG2E = math.log2(math.e)

W_GLU = 2 * CONV_DIM
W_DQK = DIFF_HEADS * 2 * DIFF_HEAD
W_DV = DIFF_HEADS * DIFF_V
W_GATE = N_BRANCH * D_MODEL
W_SMALL = MLA_Q_RANK + MLA_KV_RANK + 2 * LANES
FF_CHUNK = 256
N_FF_CHUNK = D_FF // FF_CHUNK
CONV_HALO = 32
MLA_FLASH_UNROLL = 6
DIFF_FLASH_UNROLL = 2


def _cparams(*sem):
    return pltpu.CompilerParams(dimension_semantics=sem, vmem_limit_bytes=VMEM_LIMIT)


def _rms(x, g):
    return x * lax.rsqrt(jnp.mean(x * x, axis=-1, keepdims=True) + EPS) * g


def _dot(a, b):
    return jnp.dot(a, b, preferred_element_type=F32)


def _dot_nt(a, b):
    return lax.dot_general(a, b, (((1,), (1,)), ((), ())), preferred_element_type=F32)


def _layer_block(stacked, l):
    zeros = (0,) * (stacked.ndim - 1)
    return pl.BlockSpec((None,) + stacked.shape[1:], lambda *_: (l,) + zeros, pipeline_mode=pl.Buffered(1))


def _mla_heads(sm, cos, sin, gq_ref, gkv_ref, wqa_ref, wqb_ref, wk_ref, wv_ref, q_ref, k_ref, v_ref):
    cq = sm[:, :MLA_Q_RANK]
    ckv = sm[:, MLA_Q_RANK:MLA_Q_RANK + MLA_KV_RANK]
    kr = sm[:, MLA_Q_RANK + MLA_KV_RANK:MLA_Q_RANK + MLA_KV_RANK + LANES]
    kr_rot = sm[:, MLA_Q_RANK + MLA_KV_RANK + LANES:]
    hq = _rms(cq, gq_ref[...]).astype(BF16)
    hkv = _rms(ckv, gkv_ref[...]).astype(BF16)
    qa = _dot(hq, wqa_ref[...])
    qb = _dot(hq, wqb_ref[...])
    kn = _dot(hkv, wk_ref[...])
    vv = _dot(hkv, wv_ref[...])
    kpe = kr * cos + kr_rot * sin
    lane = lax.broadcasted_iota(jnp.int32, (1, LANES), 1)
    scale = (MLA_NOPE + MLA_ROPE) ** -0.5 * LOG2E
    for h in range(MLA_HEADS):
        sl = slice(h * LANES, (h + 1) * LANES)
        q_ref[0, h] = ((qa[:, sl] * cos + qb[:, sl] * sin) * scale).astype(BF16)
        k_ref[0, h] = (kn[:, sl] + kpe).astype(BF16)
        ones = (lane >= MLA_V) if h % 2 == 0 else (lane < MLA_V)
        v_ref[0, h] = (vv[:, sl] + ones.astype(F32)).astype(BF16)


def _in_proj_kernel(x_ref, g_ref, w_ref, cos_ref, sin_ref, gq_ref, gkv_ref, wqa_ref, wqb_ref, wk_ref, wv_ref,
                    a_ref, dq_ref, dk_ref, dv_ref, q_ref, k_ref, v_ref):
    h = _rms(x_ref[...], g_ref[...]).astype(BF16)
    c0 = 0
    u = _dot(h, w_ref[:, c0:c0 + W_GLU])
    a_ref[...] = u[:, :CONV_DIM] * jax.nn.sigmoid(u[:, CONV_DIM:])
    c0 += W_GLU
    _mla_heads(_dot(h, w_ref[:, c0:c0 + W_SMALL]), cos_ref[...], sin_ref[...], gq_ref, gkv_ref,
               wqa_ref, wqb_ref, wk_ref, wv_ref, q_ref, k_ref, v_ref)
    c0 += W_SMALL
    dq_ref[...] = (_dot(h, w_ref[:, c0:c0 + W_DQK]) * (DIFF_HEAD ** -0.5 * LOG2E)).astype(BF16)
    c0 += W_DQK
    dk_ref[...] = _dot(h, w_ref[:, c0:c0 + W_DQK]).astype(BF16)
    c0 += W_DQK
    dv_ref[...] = _dot(h, w_ref[:, c0:c0 + W_DV]).astype(BF16)


def _in_proj(l, x2, cos_t, sin_t, layer_ops, tm, b, s):
    t = x2.shape[0]
    tiles_per_seq = s // tm
    row = lambda n: pl.BlockSpec((tm, n), lambda i: (i, 0))
    heads = pl.BlockSpec((1, MLA_HEADS, tm, LANES), lambda i: (i // tiles_per_seq, 0, i % tiles_per_seq, 0))
    g, w, gq, gkv, wqa, wqb, wk, wv = layer_ops
    lb = lambda a: _layer_block(a, l)
    return pl.pallas_call(
        _in_proj_kernel,
        grid=(t // tm,),
        in_specs=[row(D_MODEL), lb(g), lb(w), row(LANES), row(LANES), lb(gq), lb(gkv),
                  lb(wqa), lb(wqb), lb(wk), lb(wv)],
        out_specs=[row(CONV_DIM), row(W_DQK), row(W_DQK), row(W_DV), heads, heads, heads],
        out_shape=[jax.ShapeDtypeStruct((t, CONV_DIM), F32),
                   jax.ShapeDtypeStruct((t, W_DQK), BF16),
                   jax.ShapeDtypeStruct((t, W_DQK), BF16),
                   jax.ShapeDtypeStruct((t, W_DV), BF16)]
                  + [jax.ShapeDtypeStruct((b, MLA_HEADS, s, LANES), BF16)] * 3,
        compiler_params=_cparams("parallel"),
        name="in_proj",
    )(x2, g, w, cos_t, sin_t, gq, gkv, wqa, wqb, wk, wv)


def _score_tile(q, k, bias, s_ref, rmax_ref):
    s = _dot_nt(q, k) + bias
    s_ref[...] = s
    chunks = [s[:, c:c + LANES] for c in range(0, s.shape[1], LANES)]
    row_max = jnp.max(functools.reduce(jnp.maximum, chunks), axis=1, keepdims=True)
    rmax_ref[...] = jnp.broadcast_to(row_max, rmax_ref.shape)


def _softmax_update(s_ref, rmax_ref, v, m_ref, acc_ref):
    m_old = m_ref[...]
    m_new = jnp.maximum(m_old, rmax_ref[...])
    alpha = jnp.exp2(m_old - m_new)
    p = jnp.concatenate([jnp.exp2(s_ref[:, c:c + LANES] - m_new) for c in range(0, s_ref.shape[1], LANES)], axis=1)
    pv = _dot(p.astype(BF16), v)
    acc_ref[...] = jnp.concatenate([alpha] * (acc_ref.shape[1] // LANES), axis=1) * acc_ref[...] + pv
    m_ref[...] = m_new


def _causal_tiles(nq):
    return [(qi, kj) for qi in range(nq) for kj in range(qi + 1)]


def _tile_tables(nq):
    tiles = np.asarray(_causal_tiles(nq), dtype=np.int32)
    return jnp.asarray(tiles[:, 0]), jnp.asarray(tiles[:, 1])


def _causal_flash(streams, qtab, ktab, n_tiles, tq, bias_ref, s_buf, rmax_buf, unroll):
    n_kinds = bias_ref.shape[0]

    def blk(ref, j):
        return ref[pl.ds(pl.multiple_of(j * tq, tq), tq), :]

    def scores(t, dst):
        qi, kj = qtab[t], ktab[t]
        bias = bias_ref[jnp.minimum(qi - kj, n_kinds - 1)]
        for si, (q_ref, k_ref, _, _, _) in enumerate(streams):
            _score_tile(blk(q_ref, qi), blk(k_ref, kj), bias, s_buf.at[dst, si], rmax_buf.at[dst, si])

    def update(t, src):
        qi, kj = qtab[t], ktab[t]
        for si, (_, _, v_ref, m_ref, acc_ref) in enumerate(streams):
            _softmax_update(s_buf.at[src, si], rmax_buf.at[src, si], blk(v_ref, kj), m_ref.at[qi], acc_ref.at[qi])

    def group(jj, carry):
        for u in range(unroll):
            t = unroll * jj + u
            scores(t + 1, (u + 1) % 2)
            update(t, u % 2)
        return carry

    for _, _, _, m_ref, acc_ref in streams:
        m_ref[...] = jnp.full(m_ref.shape, NEG_BIG, F32)
        acc_ref[...] = jnp.zeros(acc_ref.shape, F32)
    scores(0, 0)
    assert unroll % 2 == 0
    n_groups = (n_tiles - 1) // unroll
    lax.fori_loop(0, n_groups, group, 0)
    for t in range(n_groups * unroll, n_tiles - 1):
        scores(t + 1, (t + 1) % 2)
        update(t, t % 2)
    update(n_tiles - 1, (n_tiles - 1) % 2)


def _mla_attn_kernel(qtab, ktab, q_ref, k_ref, v_ref, o_ref, m_ref, acc_ref, s_buf, rmax_buf, mask_ref,
                     *, tq, nq):
    @pl.when((pl.program_id(0) == 0) & (pl.program_id(1) == 0))
    def _():
        row = lax.broadcasted_iota(jnp.int32, (tq, tq), 0)
        col = lax.broadcasted_iota(jnp.int32, (tq, tq), 1)
        mask_ref[0] = jnp.where(row >= col, 0.0, NEG_BIG).astype(F32)
        mask_ref[1] = jnp.zeros((tq, tq), F32)

    streams = [(q_ref.at[0, hh], k_ref.at[0, hh], v_ref.at[0, hh], m_ref.at[hh], acc_ref.at[hh])
               for hh in range(2)]
    _causal_flash(streams, qtab, ktab, nq * (nq + 1) // 2, tq, mask_ref, s_buf, rmax_buf, MLA_FLASH_UNROLL)
    low = lax.broadcasted_iota(jnp.int32, (1, LANES), 1) < MLA_V
    for qi in range(nq):
        even, odd = acc_ref[0, qi], acc_ref[1, qi]
        num = jnp.where(low, even, odd)
        den = pltpu.roll(jnp.where(low, odd, even), MLA_V, axis=1)
        o_ref[0, qi * tq:(qi + 1) * tq, :] = (num / den).astype(BF16)


def _mla_attn(q, k, v, tq):
    b, nh, s, _ = q.shape
    nq = s // tq
    qtab, ktab = _tile_tables(nq)
    smem = pl.BlockSpec(memory_space=pltpu.SMEM)
    seq = pl.BlockSpec((1, 2, s, LANES), lambda bi, hp: (bi, hp, 0, 0))
    return pl.pallas_call(
        functools.partial(_mla_attn_kernel, tq=tq, nq=nq),
        grid=(b, nh // 2),
        in_specs=[smem, smem, seq, seq, seq],
        out_specs=pl.BlockSpec((1, s, LANES), lambda bi, hp: (bi, 0, hp)),
        out_shape=jax.ShapeDtypeStruct((b, s, nh * MLA_V), BF16),
        scratch_shapes=[pltpu.VMEM((2, nq, tq, LANES), F32), pltpu.VMEM((2, nq, tq, LANES), F32),
                        pltpu.VMEM((2, 2, tq, tq), F32), pltpu.VMEM((2, 2, tq, LANES), F32),
                        pltpu.VMEM((2, tq, tq), F32)],
        compiler_params=_cparams("arbitrary", "arbitrary"),
        name="mla_attn",
    )(qtab, ktab, q, k, v)


def _bucket_upper_bounds():
    n = np.arange(0, 4 * REL_MAX_DIST, dtype=np.int32)
    max_exact = REL_BUCKETS // 2
    nf = np.maximum(n, 1).astype(np.float32)
    large = max_exact + (np.log(nf / np.float32(max_exact)) / np.float32(math.log(REL_MAX_DIST / max_exact))
                         * np.float32(REL_BUCKETS - max_exact)).astype(np.int32)
    large = np.minimum(large, REL_BUCKETS - 1)
    bucket = np.where(n < max_exact, n, large)
    assert np.all(np.diff(bucket) >= 0) and bucket[-1] == REL_BUCKETS - 1
    return [int(np.max(n[bucket == b])) for b in range(REL_BUCKETS - 1)]


def _bias_tile_kernel(tbl_ref, o_ref, *, tq, bounds):
    h = pl.program_id(0)
    kind = pl.program_id(1)
    row = lax.broadcasted_iota(jnp.int32, (tq, tq), 0)
    col = lax.broadcasted_iota(jnp.int32, (tq, tq), 1)
    n = row - col + kind * tq
    far = tbl_ref[(REL_BUCKETS - 1) * DIFF_HEADS + h]
    r = jnp.zeros((tq, tq), F32)
    for bkt in range(REL_BUCKETS - 2, -1, -1):
        r = jnp.where(n <= bounds[bkt], (tbl_ref[bkt * DIFF_HEADS + h] - far) * LOG2E, r)
    o_ref[0, 0] = jnp.where(n >= 0, r, NEG_BIG)


def _bias_tiles(rel_bias, tq):
    bounds = _bucket_upper_bounds()
    assert bounds[-1] < tq, "bias must be constant from the second key block before the diagonal on"
    return pl.pallas_call(
        functools.partial(_bias_tile_kernel, tq=tq, bounds=bounds),
        grid=(DIFF_HEADS, 3),
        in_specs=[pl.BlockSpec(memory_space=pltpu.SMEM)],
        out_specs=pl.BlockSpec((1, 1, tq, tq), lambda h, kd: (h, kd, 0, 0)),
        out_shape=jax.ShapeDtypeStruct((DIFF_HEADS, 3, tq, tq), F32),
        compiler_params=_cparams("parallel", "parallel"),
        name="rel_bias_tiles",
    )(rel_bias.reshape(-1))


def _diff_attn_kernel(qtab, ktab, q_ref, k_ref, v_ref, bias_ref, lq1_ref, lk1_ref, lq2_ref, lk2_ref, gsub_ref,
                      o_ref, m_ref, acc_ref, s_buf, rmax_buf, qm_ref, vx_ref, *, tq, nq, lam_init):
    lam = (jnp.exp(jnp.sum(lq1_ref[...] * lk1_ref[...], axis=1, keepdims=True))
           - jnp.exp(jnp.sum(lq2_ref[...] * lk2_ref[...], axis=1, keepdims=True)) + lam_init)
    q = q_ref[0]
    lane = lax.broadcasted_iota(jnp.int32, (1, LANES), 1)
    zero = jnp.zeros_like(q)
    qm_ref[0] = jnp.where(lane < DIFF_HEAD, q, zero)
    qm_ref[1] = jnp.where(lane >= DIFF_HEAD, q, zero)
    vx_ref[:, :DIFF_V] = v_ref[0]
    vx_ref[:, DIFF_V:] = jnp.ones((vx_ref.shape[0], LANES), BF16)
    streams = [(qm_ref.at[mp], k_ref.at[0], vx_ref, m_ref.at[mp], acc_ref.at[mp]) for mp in range(2)]
    _causal_flash(streams, qtab, ktab, nq * (nq + 1) // 2, tq, bias_ref.at[0], s_buf, rmax_buf, DIFF_FLASH_UNROLL)
    for qi in range(nq):
        outs = []
        for mp in range(2):
            acc = acc_ref[mp, qi]
            outs.append(acc[:, :DIFF_V] / acc[:, DIFF_V:])
        o = outs[0] - lam * outs[1]
        o_ref[0, qi * tq:(qi + 1) * tq, :] = (_rms(o, gsub_ref[...]) * (1.0 - lam_init)).astype(BF16)


def _diff_attn(l, dq, dk, dv, bias_tiles, layer_ops, lam_init, tq):
    b, s, _ = dq.shape
    nq = s // tq
    qtab, ktab = _tile_tables(nq)
    smem = pl.BlockSpec(memory_space=pltpu.SMEM)
    seq = pl.BlockSpec((1, s, LANES), lambda bi, h: (bi, 0, h))
    n_kinds = bias_tiles.shape[1]
    return pl.pallas_call(
        functools.partial(_diff_attn_kernel, tq=tq, nq=nq, lam_init=lam_init),
        grid=(b, DIFF_HEADS),
        in_specs=[smem, smem, seq, seq, seq,
                  pl.BlockSpec((1, n_kinds, tq, tq), lambda bi, h: (h, 0, 0, 0))]
                 + [_layer_block(a, l) for a in layer_ops],
        out_specs=seq,
        out_shape=jax.ShapeDtypeStruct((b, s, W_DV), BF16),
        scratch_shapes=[pltpu.VMEM((2, nq, tq, LANES), F32), pltpu.VMEM((2, nq, tq, 2 * LANES), F32),
                        pltpu.VMEM((2, 2, tq, tq), F32), pltpu.VMEM((2, 2, tq, LANES), F32),
                        pltpu.VMEM((2, s, LANES), BF16), pltpu.VMEM((s, 2 * LANES), BF16)],
        compiler_params=_cparams("parallel", "parallel"),
        name="diff_attn",
    )(qtab, ktab, dq, dk, dv, bias_tiles, *layer_ops)


def _conv_ln_silu(a, halo, w_ref, b_ref, lng_ref, lnb_ref, buf_ref, shift_ref, tm):
    buf_ref[0:CONV_HALO, :] = halo
    buf_ref[CONV_HALO:, :] = a
    n_rows = tm + CONV_HALO - SUBLANES
    for r in range(1, SUBLANES):
        shift_ref[r - 1] = buf_ref[r:r + n_rows, :]
    base = CONV_HALO - (CONV_WIDTH - 1)
    y = jnp.zeros((tm, CONV_DIM), F32) + b_ref[...]
    for j in range(CONV_WIDTH):
        grp, r = divmod(base + j, SUBLANES)
        lo = grp * SUBLANES
        tap = buf_ref[lo:lo + tm, :] if r == 0 else shift_ref[r - 1, lo:lo + tm, :]
        y = y + w_ref[j:j + 1, :] * tap
    mu = jnp.mean(y, axis=-1, keepdims=True)
    var = jnp.mean(jnp.square(y - mu), axis=-1, keepdims=True)
    z = (y - mu) * lax.rsqrt(var + EPS) * lng_ref[...] + lnb_ref[...]
    return (z * jax.nn.sigmoid(z)).astype(BF16)


def _merge_kernel(x_ref, g_ref, wg_ref, bg_ref, a_ref, halo_ref, cw_ref, cb_ref, lng_ref, lnb_ref,
                  yb_ref, yc_ref, wa_ref, wb_ref, wc_ref, wo_ref, o_ref, buf_ref, shift_ref, *, tm, tiles_per_seq):
    x = x_ref[...]
    h = _rms(x, g_ref[...]).astype(BF16)
    halo = halo_ref[...]
    starts_seq = (pl.program_id(0) % tiles_per_seq) == 0
    ya = _conv_ln_silu(a_ref[...], jnp.where(starts_seq, jnp.zeros_like(halo), halo), cw_ref, cb_ref, lng_ref,
                       lnb_ref, buf_ref, shift_ref, tm)
    merged = None
    for br, (y, w_ref) in enumerate(((ya, wa_ref), (yb_ref[...], wb_ref), (yc_ref[...], wc_ref))):
        sl = slice(br * D_MODEL, (br + 1) * D_MODEL)
        gate = jax.nn.sigmoid(_dot(h, wg_ref[:, sl]) + bg_ref[:, sl])
        term = gate * _dot(y, w_ref[...])
        merged = term if merged is None else merged + term
    o_ref[...] = x + _dot(merged.astype(BF16), wo_ref[...])


def _merge(l, x2, a2, yb, yc, layer_ops, tm, tiles_per_seq):
    t = x2.shape[0]
    per = tm // CONV_HALO
    row = lambda w: pl.BlockSpec((tm, w), lambda i: (i, 0))
    halo = pl.BlockSpec((CONV_HALO, CONV_DIM), lambda i: (jnp.maximum(i * per - 1, 0), 0))
    g, wg, bg, cw, cb, lng, lnb, wa, wb, wc, wo = layer_ops
    lb = lambda a: _layer_block(a, l)
    return pl.pallas_call(
        functools.partial(_merge_kernel, tm=tm, tiles_per_seq=tiles_per_seq),
        grid=(t // tm,),
        in_specs=[row(D_MODEL), lb(g), lb(wg), lb(bg), row(CONV_DIM), halo, lb(cw), lb(cb), lb(lng),
                  lb(lnb), row(MLA_HEADS * MLA_V), row(W_DV), lb(wa), lb(wb), lb(wc), lb(wo)],
        out_specs=row(D_MODEL),
        out_shape=jax.ShapeDtypeStruct((t, D_MODEL), F32),
        scratch_shapes=[pltpu.VMEM((tm + CONV_HALO, CONV_DIM), F32),
                        pltpu.VMEM((SUBLANES - 1, tm + CONV_HALO - SUBLANES, CONV_DIM), F32)],
        compiler_params=_cparams("parallel"),
        name="merge",
    )(x2, g, wg, bg, a2, a2, cw, cb, lng, lnb, yb, yc, wa, wb, wc, wo)


def _ffn_kernel(x_ref, g_ref, wv_ref, wg_ref, cw_ref, cb_ref, wd_ref, gf_ref, o_ref,
                h_ref, buf_ref, carry_ref, *, tm, tiles_per_seq, final_norm):
    i = pl.program_id(0)
    x = x_ref[...]
    h_ref[...] = _rms(x, g_ref[...]).astype(BF16)
    o_ref[...] = x
    pad = SUBLANES

    @pl.when((i % tiles_per_seq) == 0)
    def _():
        carry_ref[...] = jnp.zeros_like(carry_ref)

    def up(c, slot):
        h = h_ref[...]
        buf_ref[slot, 0, pad:, :] = _dot(h, wv_ref[c])
        buf_ref[slot, 1, pad:, :] = _dot(h, wg_ref[c])

    def conv3(c, slot, half):
        buf = buf_ref.at[slot, half]
        buf[0:pad, :] = carry_ref[c, half]
        carry_ref[c, half] = buf[tm:tm + pad, :]
        w = cw_ref[c, half]
        return (w[0:1, :] * buf[pad - 2:pad - 2 + tm, :] + w[1:2, :] * buf[pad - 1:pad - 1 + tm, :]
                + w[2:3, :] * buf[pad:pad + tm, :] + cb_ref[c, half])

    def down(c, slot):
        uv = conv3(c, slot, 0)
        ug = conv3(c, slot, 1)
        act = (uv * jax.nn.sigmoid(uv) * ug).astype(BF16)
        o_ref[...] += _dot(act, wd_ref[c])

    def pair(cc, carry):
        c = 2 * cc
        up(c + 1, 1)
        down(c, 0)
        up(c + 2, 0)
        down(c + 1, 1)
        return carry

    up(0, 0)
    n_pairs = (N_FF_CHUNK - 1) // 2
    lax.fori_loop(0, n_pairs, pair, 0)
    if (N_FF_CHUNK - 1) % 2 == 1:
        up(N_FF_CHUNK - 1, 1)
        down(N_FF_CHUNK - 2, 0)
        down(N_FF_CHUNK - 1, 1)
    else:
        down(N_FF_CHUNK - 1, 0)
    if final_norm:
        o_ref[...] = _rms(o_ref[...], gf_ref[...])


def _ffn(l, x2, layer_ops, gf, tm, tiles_per_seq, final_norm):
    t = x2.shape[0]
    row = pl.BlockSpec((tm, D_MODEL), lambda i: (i, 0))
    return pl.pallas_call(
        functools.partial(_ffn_kernel, tm=tm, tiles_per_seq=tiles_per_seq, final_norm=final_norm),
        grid=(t // tm,),
        in_specs=[row] + [_layer_block(a, l) for a in layer_ops]
                 + [pl.BlockSpec(gf.shape, lambda i: (0, 0), pipeline_mode=pl.Buffered(1))],
        out_specs=row,
        out_shape=jax.ShapeDtypeStruct((t, D_MODEL), F32),
        scratch_shapes=[pltpu.VMEM((tm, D_MODEL), BF16),
                        pltpu.VMEM((2, 2, tm + SUBLANES, FF_CHUNK), F32),
                        pltpu.VMEM((N_FF_CHUNK, 2, SUBLANES, FF_CHUNK), F32)],
        compiler_params=_cparams("arbitrary"),
        name="ffn",
    )(x2, *layer_ops, gf)


def _rot_cols(w):
    half = w.shape[-1] // 2
    return jnp.concatenate([-w[..., half:], w[..., :half]], axis=-1)


def _prep_in_proj_weight(w_in):
    cuts = np.cumsum([W_GLU, MLA_Q_RANK, MLA_KV_RANK, MLA_ROPE, W_DQK, W_DQK, W_DV])
    glu, cq, ckv, kr, dq, dk, dv, gates = jnp.split(w_in, cuts, axis=-1)
    z = lambda n: jnp.zeros(w_in.shape[:-1] + (n,), w_in.dtype)
    tail = LANES - MLA_NOPE - MLA_ROPE
    w = jnp.concatenate([glu, cq, ckv, z(MLA_NOPE), kr, z(tail), z(MLA_NOPE), _rot_cols(kr), z(tail), dq, dk, dv],
                        axis=-1)
    return w.astype(BF16), gates.astype(BF16)


def _prep_mla_weights(w_uq, w_ukv):
    dqk = MLA_NOPE + MLA_ROPE
    wq = w_uq.reshape(w_uq.shape[:-1] + (MLA_HEADS, dqk))
    pad = jnp.zeros(wq.shape[:-1] + (LANES - dqk,), wq.dtype)
    wqa = jnp.concatenate([wq, pad], axis=-1)
    wqb = jnp.concatenate([jnp.zeros_like(wq[..., :MLA_NOPE]), _rot_cols(wq[..., MLA_NOPE:]), pad], axis=-1)
    wkv = w_ukv.reshape(w_ukv.shape[:-1] + (MLA_HEADS, MLA_NOPE + MLA_V))
    zk = jnp.zeros(wkv.shape[:-1] + (LANES - MLA_NOPE,), wkv.dtype)
    wk = jnp.concatenate([wkv[..., :MLA_NOPE], zk], axis=-1)
    zv = jnp.zeros(wkv.shape[:-1] + (LANES - MLA_V,), wkv.dtype)
    v_even = jnp.concatenate([wkv[..., MLA_NOPE:], zv], axis=-1)
    v_odd = jnp.concatenate([zv, wkv[..., MLA_NOPE:]], axis=-1)
    odd = (jnp.arange(MLA_HEADS) % 2 == 1)[:, None]
    wv = jnp.where(odd, v_odd, v_even)
    flat = lambda a: a.reshape(a.shape[:-2] + (MLA_HEADS * LANES,)).astype(BF16)
    return flat(wqa), flat(wqb), flat(wk), flat(wv)


def _rope_tables(positions):
    half = MLA_ROPE // 2
    freqs = ROPE_THETA ** (-jnp.arange(half, dtype=F32) / half)
    ang = positions.astype(F32)[..., None] * freqs
    cos, sin = jnp.cos(ang), jnp.sin(ang)
    b, s = positions.shape
    ones = jnp.ones((b, s, MLA_NOPE), F32)
    zeros_n = jnp.zeros((b, s, MLA_NOPE), F32)
    zeros_t = jnp.zeros((b, s, LANES - MLA_NOPE - MLA_ROPE), F32)
    cos_t = jnp.concatenate([ones, cos, cos, zeros_t], axis=-1)
    sin_t = jnp.concatenate([zeros_n, sin, sin, zeros_t], axis=-1)
    return cos_t, sin_t


def _chunk_cols(w):
    return jnp.moveaxis(w.reshape(w.shape[:-1] + (-1, FF_CHUNK)), -2, -3)


def kernel(x, positions, rel_bias, norm_mix, w_in, gate_bias, conv_w, conv_b, conv_ln_g, conv_ln_b, w_conv_out,
           mla_q_norm, w_uq, mla_kv_norm, w_ukv, w_mla_out, diff_lam_q1, diff_lam_k1, diff_lam_q2, diff_lam_k2,
           diff_sub_norm, w_diff_out, w_out, norm_ffn, w_up, ffn_conv_w, ffn_conv_b, w_down, norm_final):
    b, s, d = x.shape
    depth = w_in.shape[0]
    assert d == D_MODEL
    tm = min(512, s)
    tm_ffn = min(1024, s)
    tq = min(512, s)
    assert s % tm == 0 and s % tm_ffn == 0 and s % tq == 0 and tm % CONV_HALO == 0
    t = b * s
    vec = lambda a: a.reshape(depth, 1, -1)
    bf = lambda a: a.astype(BF16)

    cos_t, sin_t = (tbl.reshape(t, LANES) for tbl in _rope_tables(positions))
    bias_tiles = _bias_tiles(rel_bias, tq)

    w_in_p, w_gate = _prep_in_proj_weight(w_in)
    wqa, wqb, wk, wv = _prep_mla_weights(w_uq, w_ukv)
    in_proj_ops = (vec(norm_mix), w_in_p, vec(mla_q_norm), vec(mla_kv_norm), wqa, wqb, wk, wv)
    diff_ops = (vec(diff_lam_q1), vec(diff_lam_k1), vec(diff_lam_q2), vec(diff_lam_k2), vec(diff_sub_norm))
    merge_ops = (vec(norm_mix), w_gate, vec(gate_bias), conv_w, vec(conv_b), vec(conv_ln_g), vec(conv_ln_b),
                 bf(w_conv_out), bf(w_mla_out), bf(w_diff_out), bf(w_out))
    w_up_b = bf(w_up)
    ffn_ops = (vec(norm_ffn), _chunk_cols(w_up_b[..., :D_FF]), _chunk_cols(w_up_b[..., D_FF:]),
               ffn_conv_w.reshape(depth, FFN_CONV, 2, N_FF_CHUNK, FF_CHUNK).transpose(0, 3, 2, 1, 4),
               ffn_conv_b.reshape(depth, 2, N_FF_CHUNK, 1, FF_CHUNK).transpose(0, 2, 1, 3, 4),
               bf(w_down).reshape(depth, N_FF_CHUNK, FF_CHUNK, D_MODEL))

    x2 = x.reshape(t, d)
    for l in range(depth):
        a, dq, dk, dv, q, k, v = _in_proj(l, x2, cos_t, sin_t, in_proj_ops, tm, b, s)
        y_b = _mla_attn(q, k, v, tq)
        lam_init = 0.8 - 0.6 * math.exp(-0.3 * l)
        y_c = _diff_attn(l, dq.reshape(b, s, W_DQK), dk.reshape(b, s, W_DQK), dv.reshape(b, s, W_DV), bias_tiles,
                         diff_ops, lam_init, tq)
        x2 = _merge(l, x2, a, y_b.reshape(t, MLA_HEADS * MLA_V), y_c.reshape(t, W_DV), merge_ops, tm, s // tm)
        x2 = _ffn(l, x2, ffn_ops, norm_final.reshape(1, -1), tm_ffn, s // tm_ffn, l == depth - 1)
    return x2.reshape(b, s, d)
```

```python
import functools
import math

import numpy as np
import jax
import jax.numpy as jnp
from jax import lax
from jax.experimental import pallas as pl
from jax.experimental.pallas import tpu as pltpu

F32 = jnp.float32
BF16 = jnp.bfloat16

D_MODEL = 1024
CONV_DIM = 512
CONV_WIDTH = 31
MLA_HEADS = 8
MLA_NOPE = 64
MLA_ROPE = 32
MLA_V = 64
MLA_Q_RANK = 256
MLA_KV_RANK = 128
ROPE_THETA = 10000.0
DIFF_HEADS = 4
DIFF_HEAD = 64
DIFF_V = 2 * DIFF_HEAD
REL_BUCKETS = 32
REL_MAX_DIST = 128
D_FF = 2816
FFN_CONV = 3
N_BRANCH = 3
EPS = 1e-6

LANES = 128
SUBLANES = 8
VMEM_LIMIT = 56 * 1024 * 1024
NEG_BIG = -1e30
LOG2E = math.log2(math.e)

W_GLU = 2 * CONV_DIM
W_DQK = DIFF_HEADS * 2 * DIFF_HEAD
W_DV = DIFF_HEADS * DIFF_V
W_GATE = N_BRANCH * D_MODEL
W_SMALL = MLA_Q_RANK + MLA_KV_RANK + 2 * LANES
FF_CHUNK = 256
N_FF_CHUNK = D_FF // FF_CHUNK
CONV_HALO = 32
MLA_FLASH_UNROLL = 6
DIFF_FLASH_UNROLL = 2


def _cparams(*sem):
    return pltpu.CompilerParams(dimension_semantics=sem, vmem_limit_bytes=VMEM_LIMIT)


def _rms(x, g):
    return x * lax.rsqrt(jnp.mean(x * x, axis=-1, keepdims=True) + EPS) * g


def _dot(a, b):
    return jnp.dot(a, b, preferred_element_type=F32)


def _dot_nt(a, b):
    return lax.dot_general(a, b, (((1,), (1,)), ((), ())), preferred_element_type=F32)


def _layer_block(stacked, l):
    zeros = (0,) * (stacked.ndim - 1)
    return pl.BlockSpec((None,) + stacked.shape[1:], lambda *_: (l,) + zeros, pipeline_mode=pl.Buffered(1))


def _mla_heads(sm, cos, sin, gq_ref, gkv_ref, wqa_ref, wqb_ref, wk_ref, wv_ref, q_ref, k_ref, v_ref):
    cq = sm[:, :MLA_Q_RANK]
    ckv = sm[:, MLA_Q_RANK:MLA_Q_RANK + MLA_KV_RANK]
    kr = sm[:, MLA_Q_RANK + MLA_KV_RANK:MLA_Q_RANK + MLA_KV_RANK + LANES]
    kr_rot = sm[:, MLA_Q_RANK + MLA_KV_RANK + LANES:]
    hq = _rms(cq, gq_ref[...]).astype(BF16)
    hkv = _rms(ckv, gkv_ref[...]).astype(BF16)
    qa = _dot(hq, wqa_ref[...])
    qb = _dot(hq, wqb_ref[...])
    kn = _dot(hkv, wk_ref[...])
    vv = _dot(hkv, wv_ref[...])
    kpe = kr * cos + kr_rot * sin
    lane = lax.broadcasted_iota(jnp.int32, (1, LANES), 1)
    scale = (MLA_NOPE + MLA_ROPE) ** -0.5 * LOG2E
    for h in range(MLA_HEADS):
        sl = slice(h * LANES, (h + 1) * LANES)
        q_ref[0, h] = ((qa[:, sl] * cos + qb[:, sl] * sin) * scale).astype(BF16)
        k_ref[0, h] = (kn[:, sl] + kpe).astype(BF16)
        ones = (lane >= MLA_V) if h % 2 == 0 else (lane < MLA_V)
        v_ref[0, h] = (vv[:, sl] + ones.astype(F32)).astype(BF16)


def _in_proj_kernel(x_ref, g_ref, w_ref, cos_ref, sin_ref, gq_ref, gkv_ref, wqa_ref, wqb_ref, wk_ref, wv_ref,
                    a_ref, dq_ref, dk_ref, dv_ref, q_ref, k_ref, v_ref):
    h = _rms(x_ref[...], g_ref[...]).astype(BF16)
    c0 = 0
    u = _dot(h, w_ref[:, c0:c0 + W_GLU])
    a_ref[...] = u[:, :CONV_DIM] * jax.nn.sigmoid(u[:, CONV_DIM:])
    c0 += W_GLU
    _mla_heads(_dot(h, w_ref[:, c0:c0 + W_SMALL]), cos_ref[...], sin_ref[...], gq_ref, gkv_ref,
               wqa_ref, wqb_ref, wk_ref, wv_ref, q_ref, k_ref, v_ref)
    c0 += W_SMALL
    dq_ref[...] = (_dot(h, w_ref[:, c0:c0 + W_DQK]) * (DIFF_HEAD ** -0.5 * LOG2E)).astype(BF16)
    c0 += W_DQK
    dk_ref[...] = _dot(h, w_ref[:, c0:c0 + W_DQK]).astype(BF16)
    c0 += W_DQK
    dv_ref[...] = _dot(h, w_ref[:, c0:c0 + W_DV]).astype(BF16)


def _in_proj(l, x2, cos_t, sin_t, layer_ops, tm, b, s):
    t = x2.shape[0]
    tiles_per_seq = s // tm
    row = lambda n: pl.BlockSpec((tm, n), lambda i: (i, 0))
    heads = pl.BlockSpec((1, MLA_HEADS, tm, LANES), lambda i: (i // tiles_per_seq, 0, i % tiles_per_seq, 0))
    g, w, gq, gkv, wqa, wqb, wk, wv = layer_ops
    lb = lambda a: _layer_block(a, l)
    return pl.pallas_call(
        _in_proj_kernel,
        grid=(t // tm,),
        in_specs=[row(D_MODEL), lb(g), lb(w), row(LANES), row(LANES), lb(gq), lb(gkv),
                  lb(wqa), lb(wqb), lb(wk), lb(wv)],
        out_specs=[row(CONV_DIM), row(W_DQK), row(W_DQK), row(W_DV), heads, heads, heads],
        out_shape=[jax.ShapeDtypeStruct((t, CONV_DIM), F32),
                   jax.ShapeDtypeStruct((t, W_DQK), BF16),
                   jax.ShapeDtypeStruct((t, W_DQK), BF16),
                   jax.ShapeDtypeStruct((t, W_DV), BF16)]
                  + [jax.ShapeDtypeStruct((b, MLA_HEADS, s, LANES), BF16)] * 3,
        compiler_params=_cparams("parallel"),
        name="in_proj",
    )(x2, g, w, cos_t, sin_t, gq, gkv, wqa, wqb, wk, wv)


def _score_tile(q, k, bias, s_ref, rmax_ref):
    s = _dot_nt(q, k)
    if bias is not None:
        s = s + bias
    s_ref[...] = s
    chunks = [s[:, c:c + LANES] for c in range(0, s.shape[1], LANES)]
    row_max = jnp.max(functools.reduce(jnp.maximum, chunks), axis=1, keepdims=True)
    rmax_ref[...] = jnp.broadcast_to(row_max, rmax_ref.shape)


def _softmax_update(s_ref, rmax_ref, v, m_ref, acc_ref):
    m_old = m_ref[...]
    m_new = jnp.maximum(m_old, rmax_ref[...])
    alpha = jnp.exp2(m_old - m_new)
    p = jnp.concatenate([jnp.exp2(s_ref[:, c:c + LANES] - m_new) for c in range(0, s_ref.shape[1], LANES)], axis=1)
    pv = _dot(p.astype(BF16), v)
    acc_ref[...] = jnp.concatenate([alpha] * (acc_ref.shape[1] // LANES), axis=1) * acc_ref[...] + pv
    m_ref[...] = m_new


def _tile_tables(nq, n_biased):
    tiles = [(qi, kj) for qi in range(nq) for kj in range(qi + 1)]
    near = [t for t in tiles if t[0] - t[1] < n_biased]
    far = [t for t in tiles if t[0] - t[1] >= n_biased]
    order = np.asarray(near + far, dtype=np.int32)
    return jnp.asarray(order[:, 0]), jnp.asarray(order[:, 1]), len(near)


def _causal_flash(streams, qtab, ktab, segments, tq, bias_ref, s_buf, rmax_buf):
    def blk(ref, j):
        return ref[pl.ds(pl.multiple_of(j * tq, tq), tq), :]

    def update(t, src):
        qi, kj = qtab[t], ktab[t]
        for si, (_, _, v_ref, m_ref, acc_ref) in enumerate(streams):
            _softmax_update(s_buf.at[src, si], rmax_buf.at[src, si], blk(v_ref, kj), m_ref.at[qi], acc_ref.at[qi])

    def run(first, n_tiles, biased, unroll):
        def scores(t, dst):
            qi, kj = qtab[first + t], ktab[first + t]
            bias = bias_ref[qi - kj] if biased else None
            for si, (q_ref, k_ref, _, _, _) in enumerate(streams):
                _score_tile(blk(q_ref, qi), blk(k_ref, kj), bias, s_buf.at[dst, si], rmax_buf.at[dst, si])

        def group(jj, carry):
            for u in range(unroll):
                t = unroll * jj + u
                scores(t + 1, (u + 1) % 2)
                update(first + t, u % 2)
            return carry

        assert unroll % 2 == 0
        scores(0, 0)
        n_groups = (n_tiles - 1) // unroll
        lax.fori_loop(0, n_groups, group, 0)
        for t in range(n_groups * unroll, n_tiles - 1):
            scores(t + 1, (t + 1) % 2)
            update(first + t, t % 2)
        update(first + n_tiles - 1, (n_tiles - 1) % 2)

    for _, _, _, m_ref, acc_ref in streams:
        m_ref[...] = jnp.full(m_ref.shape, NEG_BIG, F32)
        acc_ref[...] = jnp.zeros(acc_ref.shape, F32)
    for segment in segments:
        if segment[1] > 0:
            run(*segment)


def _mla_attn_kernel(qtab, ktab, q_ref, k_ref, v_ref, o_ref, m_ref, acc_ref, s_buf, rmax_buf, mask_ref,
                     *, tq, nq, segments):
    @pl.when((pl.program_id(0) == 0) & (pl.program_id(1) == 0))
    def _():
        row = lax.broadcasted_iota(jnp.int32, (tq, tq), 0)
        col = lax.broadcasted_iota(jnp.int32, (tq, tq), 1)
        mask_ref[0] = jnp.where(row >= col, 0.0, NEG_BIG).astype(F32)

    streams = [(q_ref.at[0, hh], k_ref.at[0, hh], v_ref.at[0, hh], m_ref.at[hh], acc_ref.at[hh])
               for hh in range(2)]
    _causal_flash(streams, qtab, ktab, segments, tq, mask_ref, s_buf, rmax_buf)
    low = lax.broadcasted_iota(jnp.int32, (1, LANES), 1) < MLA_V
    for qi in range(nq):
        even, odd = acc_ref[0, qi], acc_ref[1, qi]
        num = jnp.where(low, even, odd)
        den = pltpu.roll(jnp.where(low, odd, even), MLA_V, axis=1)
        o_ref[0, qi * tq:(qi + 1) * tq, :] = (num / den).astype(BF16)


def _mla_attn(q, k, v, tq):
    b, nh, s, _ = q.shape
    nq = s // tq
    qtab, ktab, n_near = _tile_tables(nq, 1)
    n_tiles = qtab.shape[0]
    segments = ((0, n_near, True, 2), (n_near, n_tiles - n_near, False, MLA_FLASH_UNROLL))
    smem = pl.BlockSpec(memory_space=pltpu.SMEM)
    seq = pl.BlockSpec((1, 2, s, LANES), lambda bi, hp: (bi, hp, 0, 0))
    return pl.pallas_call(
        functools.partial(_mla_attn_kernel, tq=tq, nq=nq, segments=segments),
        grid=(b, nh // 2),
        in_specs=[smem, smem, seq, seq, seq],
        out_specs=pl.BlockSpec((1, s, LANES), lambda bi, hp: (bi, 0, hp)),
        out_shape=jax.ShapeDtypeStruct((b, s, nh * MLA_V), BF16),
        scratch_shapes=[pltpu.VMEM((2, nq, tq, LANES), F32), pltpu.VMEM((2, nq, tq, LANES), F32),
                        pltpu.VMEM((2, 2, tq, tq), F32), pltpu.VMEM((2, 2, tq, LANES), F32),
                        pltpu.VMEM((1, tq, tq), F32)],
        compiler_params=_cparams("arbitrary", "arbitrary"),
        name="mla_attn",
    )(qtab, ktab, q, k, v)


def _bucket_upper_bounds():
    n = np.arange(0, 4 * REL_MAX_DIST, dtype=np.int32)
    max_exact = REL_BUCKETS // 2
    nf = np.maximum(n, 1).astype(np.float32)
    large = max_exact + (np.log(nf / np.float32(max_exact)) / np.float32(math.log(REL_MAX_DIST / max_exact))
                         * np.float32(REL_BUCKETS - max_exact)).astype(np.int32)
    large = np.minimum(large, REL_BUCKETS - 1)
    bucket = np.where(n < max_exact, n, large)
    assert np.all(np.diff(bucket) >= 0) and bucket[-1] == REL_BUCKETS - 1
    return [int(np.max(n[bucket == b])) for b in range(REL_BUCKETS - 1)]


def _bias_tile_kernel(tbl_ref, o_ref, *, tq, bounds):
    h = pl.program_id(0)
    kind = pl.program_id(1)
    row = lax.broadcasted_iota(jnp.int32, (tq, tq), 0)
    col = lax.broadcasted_iota(jnp.int32, (tq, tq), 1)
    n = row - col + kind * tq
    far = tbl_ref[(REL_BUCKETS - 1) * DIFF_HEADS + h]
    r = jnp.zeros((tq, tq), F32)
    for bkt in range(REL_BUCKETS - 2, -1, -1):
        r = jnp.where(n <= bounds[bkt], (tbl_ref[bkt * DIFF_HEADS + h] - far) * LOG2E, r)
    o_ref[0, 0] = jnp.where(n >= 0, r, NEG_BIG)


def _bias_tiles(rel_bias, tq):
    bounds = _bucket_upper_bounds()
    assert bounds[-1] < tq, "bias must be constant from the second key block before the diagonal on"
    return pl.pallas_call(
        functools.partial(_bias_tile_kernel, tq=tq, bounds=bounds),
        grid=(DIFF_HEADS, 2),
        in_specs=[pl.BlockSpec(memory_space=pltpu.SMEM)],
        out_specs=pl.BlockSpec((1, 1, tq, tq), lambda h, kd: (h, kd, 0, 0)),
        out_shape=jax.ShapeDtypeStruct((DIFF_HEADS, 2, tq, tq), F32),
        compiler_params=_cparams("parallel", "parallel"),
        name="rel_bias_tiles",
    )(rel_bias.reshape(-1))


def _diff_attn_kernel(qtab, ktab, q_ref, k_ref, v_ref, bias_ref, lq1_ref, lk1_ref, lq2_ref, lk2_ref, gsub_ref,
                      o_ref, m_ref, acc_ref, s_buf, rmax_buf, qm_ref, vx_ref, *, tq, nq, lam_init, segments):
    lam = (jnp.exp(jnp.sum(lq1_ref[...] * lk1_ref[...], axis=1, keepdims=True))
           - jnp.exp(jnp.sum(lq2_ref[...] * lk2_ref[...], axis=1, keepdims=True)) + lam_init)
    q = q_ref[0]
    lane = lax.broadcasted_iota(jnp.int32, (1, LANES), 1)
    zero = jnp.zeros_like(q)
    qm_ref[0] = jnp.where(lane < DIFF_HEAD, q, zero)
    qm_ref[1] = jnp.where(lane >= DIFF_HEAD, q, zero)
    vx_ref[:, :DIFF_V] = v_ref[0]
    vx_ref[:, DIFF_V:] = jnp.ones((vx_ref.shape[0], LANES), BF16)
    streams = [(qm_ref.at[mp], k_ref.at[0], vx_ref, m_ref.at[mp], acc_ref.at[mp]) for mp in range(2)]
    _causal_flash(streams, qtab, ktab, segments, tq, bias_ref.at[0], s_buf, rmax_buf)
    for qi in range(nq):
        outs = []
        for mp in range(2):
            acc = acc_ref[mp, qi]
            outs.append(acc[:, :DIFF_V] / acc[:, DIFF_V:])
        o = outs[0] - lam * outs[1]
        o_ref[0, qi * tq:(qi + 1) * tq, :] = (_rms(o, gsub_ref[...]) * (1.0 - lam_init)).astype(BF16)


def _diff_attn(l, dq, dk, dv, bias_tiles, layer_ops, lam_init, tq):
    b, s, _ = dq.shape
    nq = s // tq
    n_kinds = bias_tiles.shape[1]
    qtab, ktab, n_near = _tile_tables(nq, n_kinds)
    n_tiles = qtab.shape[0]
    segments = ((0, n_near, True, DIFF_FLASH_UNROLL), (n_near, n_tiles - n_near, False, DIFF_FLASH_UNROLL))
    smem = pl.BlockSpec(memory_space=pltpu.SMEM)
    seq = pl.BlockSpec((1, s, LANES), lambda bi, h: (bi, 0, h))
    return pl.pallas_call(
        functools.partial(_diff_attn_kernel, tq=tq, nq=nq, lam_init=lam_init, segments=segments),
        grid=(b, DIFF_HEADS),
        in_specs=[smem, smem, seq, seq, seq,
                  pl.BlockSpec((1, n_kinds, tq, tq), lambda bi, h: (h, 0, 0, 0))]
                 + [_layer_block(a, l) for a in layer_ops],
        out_specs=seq,
        out_shape=jax.ShapeDtypeStruct((b, s, W_DV), BF16),
        scratch_shapes=[pltpu.VMEM((2, nq, tq, LANES), F32), pltpu.VMEM((2, nq, tq, 2 * LANES), F32),
                        pltpu.VMEM((2, 2, tq, tq), F32), pltpu.VMEM((2, 2, tq, LANES), F32),
                        pltpu.VMEM((2, s, LANES), BF16), pltpu.VMEM((s, 2 * LANES), BF16)],
        compiler_params=_cparams("parallel", "parallel"),
        name="diff_attn",
    )(qtab, ktab, dq, dk, dv, bias_tiles, *layer_ops)


def _conv_ln_silu(a, halo, w_ref, b_ref, lng_ref, lnb_ref, buf_ref, shift_ref, tm):
    buf_ref[0:CONV_HALO, :] = halo
    buf_ref[CONV_HALO:, :] = a
    n_rows = tm + CONV_HALO - SUBLANES
    for r in range(1, SUBLANES):
        shift_ref[r - 1] = buf_ref[r:r + n_rows, :]
    base = CONV_HALO - (CONV_WIDTH - 1)
    y = jnp.zeros((tm, CONV_DIM), F32) + b_ref[...]
    for j in range(CONV_WIDTH):
        grp, r = divmod(base + j, SUBLANES)
        lo = grp * SUBLANES
        tap = buf_ref[lo:lo + tm, :] if r == 0 else shift_ref[r - 1, lo:lo + tm, :]
        y = y + w_ref[j:j + 1, :] * tap
    mu = jnp.mean(y, axis=-1, keepdims=True)
    var = jnp.mean(jnp.square(y - mu), axis=-1, keepdims=True)
    z = (y - mu) * lax.rsqrt(var + EPS) * lng_ref[...] + lnb_ref[...]
    return (z * jax.nn.sigmoid(z)).astype(BF16)


def _merge_kernel(x_ref, g_ref, wg_ref, bg_ref, a_ref, halo_ref, cw_ref, cb_ref, lng_ref, lnb_ref,
                  yb_ref, yc_ref, wa_ref, wb_ref, wc_ref, wo_ref, o_ref, buf_ref, shift_ref, *, tm, tiles_per_seq):
    x = x_ref[...]
    h = _rms(x, g_ref[...]).astype(BF16)
    halo = halo_ref[...]
    starts_seq = (pl.program_id(0) % tiles_per_seq) == 0
    ya = _conv_ln_silu(a_ref[...], jnp.where(starts_seq, jnp.zeros_like(halo), halo), cw_ref, cb_ref, lng_ref,
                       lnb_ref, buf_ref, shift_ref, tm)
    merged = None
    for br, (y, w_ref) in enumerate(((ya, wa_ref), (yb_ref[...], wb_ref), (yc_ref[...], wc_ref))):
        sl = slice(br * D_MODEL, (br + 1) * D_MODEL)
        gate = jax.nn.sigmoid(_dot(h, wg_ref[:, sl]) + bg_ref[:, sl])
        term = gate * _dot(y, w_ref[...])
        merged = term if merged is None else merged + term
    o_ref[...] = x + _dot(merged.astype(BF16), wo_ref[...])


def _merge(l, x2, a2, yb, yc, layer_ops, tm, tiles_per_seq):
    t = x2.shape[0]
    per = tm // CONV_HALO
    row = lambda w: pl.BlockSpec((tm, w), lambda i: (i, 0))
    halo = pl.BlockSpec((CONV_HALO, CONV_DIM), lambda i: (jnp.maximum(i * per - 1, 0), 0))
    g, wg, bg, cw, cb, lng, lnb, wa, wb, wc, wo = layer_ops
    lb = lambda a: _layer_block(a, l)
    return pl.pallas_call(
        functools.partial(_merge_kernel, tm=tm, tiles_per_seq=tiles_per_seq),
        grid=(t // tm,),
        in_specs=[row(D_MODEL), lb(g), lb(wg), lb(bg), row(CONV_DIM), halo, lb(cw), lb(cb), lb(lng),
                  lb(lnb), row(MLA_HEADS * MLA_V), row(W_DV), lb(wa), lb(wb), lb(wc), lb(wo)],
        out_specs=row(D_MODEL),
        out_shape=jax.ShapeDtypeStruct((t, D_MODEL), F32),
        scratch_shapes=[pltpu.VMEM((tm + CONV_HALO, CONV_DIM), F32),
                        pltpu.VMEM((SUBLANES - 1, tm + CONV_HALO - SUBLANES, CONV_DIM), F32)],
        compiler_params=_cparams("parallel"),
        name="merge",
    )(x2, g, wg, bg, a2, a2, cw, cb, lng, lnb, yb, yc, wa, wb, wc, wo)


def _ffn_kernel(x_ref, g_ref, wv_ref, wg_ref, cw_ref, cb_ref, wd_ref, gf_ref, o_ref,
                h_ref, buf_ref, carry_ref, *, tm, tiles_per_seq, final_norm):
    i = pl.program_id(0)
    x = x_ref[...]
    h_ref[...] = _rms(x, g_ref[...]).astype(BF16)
    o_ref[...] = x
    pad = SUBLANES

    @pl.when((i % tiles_per_seq) == 0)
    def _():
        carry_ref[...] = jnp.zeros_like(carry_ref)

    def up(c, slot):
        h = h_ref[...]
        buf_ref[slot, 0, pad:, :] = _dot(h, wv_ref[c])
        buf_ref[slot, 1, pad:, :] = _dot(h, wg_ref[c])

    def conv3(c, slot, half):
        buf = buf_ref.at[slot, half]
        buf[0:pad, :] = carry_ref[c, half]
        carry_ref[c, half] = buf[tm:tm + pad, :]
        w = cw_ref[c, half]
        return (w[0:1, :] * buf[pad - 2:pad - 2 + tm, :] + w[1:2, :] * buf[pad - 1:pad - 1 + tm, :]
                + w[2:3, :] * buf[pad:pad + tm, :] + cb_ref[c, half])

    def down(c, slot):
        uv = conv3(c, slot, 0)
        ug = conv3(c, slot, 1)
        act = (uv * jax.nn.sigmoid(uv) * ug).astype(BF16)
        o_ref[...] += _dot(act, wd_ref[c])

    def pair(cc, carry):
        c = 2 * cc
        up(c + 1, 1)
        down(c, 0)
        up(c + 2, 0)
        down(c + 1, 1)
        return carry

    up(0, 0)
    n_pairs = (N_FF_CHUNK - 1) // 2
    lax.fori_loop(0, n_pairs, pair, 0)
    if (N_FF_CHUNK - 1) % 2 == 1:
        up(N_FF_CHUNK - 1, 1)
        down(N_FF_CHUNK - 2, 0)
        down(N_FF_CHUNK - 1, 1)
    else:
        down(N_FF_CHUNK - 1, 0)
    if final_norm:
        o_ref[...] = _rms(o_ref[...], gf_ref[...])


def _ffn(l, x2, layer_ops, gf, tm, tiles_per_seq, final_norm):
    t = x2.shape[0]
    row = pl.BlockSpec((tm, D_MODEL), lambda i: (i, 0))
    return pl.pallas_call(
        functools.partial(_ffn_kernel, tm=tm, tiles_per_seq=tiles_per_seq, final_norm=final_norm),
        grid=(t // tm,),
        in_specs=[row] + [_layer_block(a, l) for a in layer_ops]
                 + [pl.BlockSpec(gf.shape, lambda i: (0, 0), pipeline_mode=pl.Buffered(1))],
        out_specs=row,
        out_shape=jax.ShapeDtypeStruct((t, D_MODEL), F32),
        scratch_shapes=[pltpu.VMEM((tm, D_MODEL), BF16),
                        pltpu.VMEM((2, 2, tm + SUBLANES, FF_CHUNK), F32),
                        pltpu.VMEM((N_FF_CHUNK, 2, SUBLANES, FF_CHUNK), F32)],
        compiler_params=_cparams("arbitrary"),
        name="ffn",
    )(x2, *layer_ops, gf)


def _rot_cols(w):
    half = w.shape[-1] // 2
    return jnp.concatenate([-w[..., half:], w[..., :half]], axis=-1)


def _prep_in_proj_weight(w_in):
    cuts = np.cumsum([W_GLU, MLA_Q_RANK, MLA_KV_RANK, MLA_ROPE, W_DQK, W_DQK, W_DV])
    glu, cq, ckv, kr, dq, dk, dv, gates = jnp.split(w_in, cuts, axis=-1)
    z = lambda n: jnp.zeros(w_in.shape[:-1] + (n,), w_in.dtype)
    tail = LANES - MLA_NOPE - MLA_ROPE
    w = jnp.concatenate([glu, cq, ckv, z(MLA_NOPE), kr, z(tail), z(MLA_NOPE), _rot_cols(kr), z(tail), dq, dk, dv],
                        axis=-1)
    return w.astype(BF16), gates.astype(BF16)


def _prep_mla_weights(w_uq, w_ukv):
    dqk = MLA_NOPE + MLA_ROPE
    wq = w_uq.reshape(w_uq.shape[:-1] + (MLA_HEADS, dqk))
    pad = jnp.zeros(wq.shape[:-1] + (LANES - dqk,), wq.dtype)
    wqa = jnp.concatenate([wq, pad], axis=-1)
    wqb = jnp.concatenate([jnp.zeros_like(wq[..., :MLA_NOPE]), _rot_cols(wq[..., MLA_NOPE:]), pad], axis=-1)
    wkv = w_ukv.reshape(w_ukv.shape[:-1] + (MLA_HEADS, MLA_NOPE + MLA_V))
    zk = jnp.zeros(wkv.shape[:-1] + (LANES - MLA_NOPE,), wkv.dtype)
    wk = jnp.concatenate([wkv[..., :MLA_NOPE], zk], axis=-1)
    zv = jnp.zeros(wkv.shape[:-1] + (LANES - MLA_V,), wkv.dtype)
    v_even = jnp.concatenate([wkv[..., MLA_NOPE:], zv], axis=-1)
    v_odd = jnp.concatenate([zv, wkv[..., MLA_NOPE:]], axis=-1)
    odd = (jnp.arange(MLA_HEADS) % 2 == 1)[:, None]
    wv = jnp.where(odd, v_odd, v_even)
    flat = lambda a: a.reshape(a.shape[:-2] + (MLA_HEADS * LANES,)).astype(BF16)
    return flat(wqa), flat(wqb), flat(wk), flat(wv)


def _rope_tables(positions):
    half = MLA_ROPE // 2
    freqs = ROPE_THETA ** (-jnp.arange(half, dtype=F32) / half)
    ang = positions.astype(F32)[..., None] * freqs
    cos, sin = jnp.cos(ang), jnp.sin(ang)
    b, s = positions.shape
    ones = jnp.ones((b, s, MLA_NOPE), F32)
    zeros_n = jnp.zeros((b, s, MLA_NOPE), F32)
    zeros_t = jnp.zeros((b, s, LANES - MLA_NOPE - MLA_ROPE), F32)
    cos_t = jnp.concatenate([ones, cos, cos, zeros_t], axis=-1)
    sin_t = jnp.concatenate([zeros_n, sin, sin, zeros_t], axis=-1)
    return cos_t, sin_t


def _chunk_cols(w):
    return jnp.moveaxis(w.reshape(w.shape[:-1] + (-1, FF_CHUNK)), -2, -3)


def kernel(x, positions, rel_bias, norm_mix, w_in, gate_bias, conv_w, conv_b, conv_ln_g, conv_ln_b, w_conv_out,
           mla_q_norm, w_uq, mla_kv_norm, w_ukv, w_mla_out, diff_lam_q1, diff_lam_k1, diff_lam_q2, diff_lam_k2,
           diff_sub_norm, w_diff_out, w_out, norm_ffn, w_up, ffn_conv_w, ffn_conv_b, w_down, norm_final):
    b, s, d = x.shape
    depth = w_in.shape[0]
    assert d == D_MODEL
    tm = min(512, s)
    tm_ffn = min(1024, s)
    tq = min(512, s)
    assert s % tm == 0 and s % tm_ffn == 0 and s % tq == 0 and tm % CONV_HALO == 0
    t = b * s
    vec = lambda a: a.reshape(depth, 1, -1)
    bf = lambda a: a.astype(BF16)

    cos_t, sin_t = (tbl.reshape(t, LANES) for tbl in _rope_tables(positions))
    bias_tiles = _bias_tiles(rel_bias, tq)

    w_in_p, w_gate = _prep_in_proj_weight(w_in)
    wqa, wqb, wk, wv = _prep_mla_weights(w_uq, w_ukv)
    in_proj_ops = (vec(norm_mix), w_in_p, vec(mla_q_norm), vec(mla_kv_norm), wqa, wqb, wk, wv)
    diff_ops = (vec(diff_lam_q1), vec(diff_lam_k1), vec(diff_lam_q2), vec(diff_lam_k2), vec(diff_sub_norm))
    merge_ops = (vec(norm_mix), w_gate, vec(gate_bias), conv_w, vec(conv_b), vec(conv_ln_g), vec(conv_ln_b),
                 bf(w_conv_out), bf(w_mla_out), bf(w_diff_out), bf(w_out))
    w_up_b = bf(w_up)
    ffn_ops = (vec(norm_ffn), _chunk_cols(w_up_b[..., :D_FF]), _chunk_cols(w_up_b[..., D_FF:]),
               ffn_conv_w.reshape(depth, FFN_CONV, 2, N_FF_CHUNK, FF_CHUNK).transpose(0, 3, 2, 1, 4),
               ffn_conv_b.reshape(depth, 2, N_FF_CHUNK, 1, FF_CHUNK).transpose(0, 2, 1, 3, 4),
               bf(w_down).reshape(depth, N_FF_CHUNK, FF_CHUNK, D_MODEL))

    x2 = x.reshape(t, d)
    for l in range(depth):
        a, dq, dk, dv, q, k, v = _in_proj(l, x2, cos_t, sin_t, in_proj_ops, tm, b, s)
        y_b = _mla_attn(q, k, v, tq)
        lam_init = 0.8 - 0.6 * math.exp(-0.3 * l)
        y_c = _diff_attn(l, dq.reshape(b, s, W_DQK), dk.reshape(b, s, W_DQK), dv.reshape(b, s, W_DV), bias_tiles,
                         diff_ops, lam_init, tq)
        x2 = _merge(l, x2, a, y_b.reshape(t, MLA_HEADS * MLA_V), y_c.reshape(t, W_DV), merge_ops, tm, s // tm)
        x2 = _ffn(l, x2, ffn_ops, norm_final.reshape(1, -1), tm_ffn, s // tm_ffn, l == depth - 1)
    return x2.reshape(b, s, d)
```

```python
import functools
import math

import numpy as np
import jax
import jax.numpy as jnp
from jax import lax
from jax.experimental import pallas as pl
from jax.experimental.pallas import tpu as pltpu

F32 = jnp.float32
BF16 = jnp.bfloat16

D_MODEL = 1024
CONV_DIM = 512
CONV_WIDTH = 31
MLA_HEADS = 8
MLA_NOPE = 64
MLA_ROPE = 32
MLA_V = 64
MLA_Q_RANK = 256
MLA_KV_RANK = 128
ROPE_THETA = 10000.0
DIFF_HEADS = 4
DIFF_HEAD = 64
DIFF_V = 2 * DIFF_HEAD
REL_BUCKETS = 32
REL_MAX_DIST = 128
D_FF = 2816
FFN_CONV = 3
N_BRANCH = 3
EPS = 1e-6

LANES = 128
SUBLANES = 8
VMEM_LIMIT = 56 * 1024 * 1024
NEG_BIG = -1e30
LOG2E = math.log2(math.e)

W_GLU = 2 * CONV_DIM
W_DQK = DIFF_HEADS * 2 * DIFF_HEAD
W_DV = DIFF_HEADS * DIFF_V
W_GATE = N_BRANCH * D_MODEL
W_SMALL = MLA_Q_RANK + MLA_KV_RANK + 2 * LANES
FF_CHUNK = 256
N_FF_CHUNK = D_FF // FF_CHUNK
CONV_HALO = 32
MLA_FLASH_UNROLL = 8
DIFF_FLASH_UNROLL = 2


def _cparams(*sem):
    return pltpu.CompilerParams(dimension_semantics=sem, vmem_limit_bytes=VMEM_LIMIT)


def _rms(x, g):
    return x * lax.rsqrt(jnp.mean(x * x, axis=-1, keepdims=True) + EPS) * g


def _dot(a, b):
    return jnp.dot(a, b, preferred_element_type=F32)


def _dot_nt(a, b):
    return lax.dot_general(a, b, (((1,), (1,)), ((), ())), preferred_element_type=F32)


def _layer_block(stacked, l):
    zeros = (0,) * (stacked.ndim - 1)
    return pl.BlockSpec((None,) + stacked.shape[1:], lambda *_: (l,) + zeros, pipeline_mode=pl.Buffered(1))


def _mla_heads(sm, cos, sin, gq_ref, gkv_ref, wqa_ref, wqb_ref, wk_ref, wv_ref, q_ref, k_ref, v_ref):
    cq = sm[:, :MLA_Q_RANK]
    ckv = sm[:, MLA_Q_RANK:MLA_Q_RANK + MLA_KV_RANK]
    kr = sm[:, MLA_Q_RANK + MLA_KV_RANK:MLA_Q_RANK + MLA_KV_RANK + LANES]
    kr_rot = sm[:, MLA_Q_RANK + MLA_KV_RANK + LANES:]
    hq = _rms(cq, gq_ref[...]).astype(BF16)
    hkv = _rms(ckv, gkv_ref[...]).astype(BF16)
    qa = _dot(hq, wqa_ref[...])
    qb = _dot(hq, wqb_ref[...])
    kn = _dot(hkv, wk_ref[...])
    vv = _dot(hkv, wv_ref[...])
    kpe = kr * cos + kr_rot * sin
    lane = lax.broadcasted_iota(jnp.int32, (1, LANES), 1)
    scale = (MLA_NOPE + MLA_ROPE) ** -0.5 * LOG2E
    for h in range(MLA_HEADS):
        sl = slice(h * LANES, (h + 1) * LANES)
        q_ref[0, h] = ((qa[:, sl] * cos + qb[:, sl] * sin) * scale).astype(BF16)
        k_ref[0, h] = (kn[:, sl] + kpe).astype(BF16)
        ones = (lane >= MLA_V) if h % 2 == 0 else (lane < MLA_V)
        v_ref[0, h] = (vv[:, sl] + ones.astype(F32)).astype(BF16)


def _in_proj_kernel(x_ref, g_ref, w_ref, cos_ref, sin_ref, gq_ref, gkv_ref, wqa_ref, wqb_ref, wk_ref, wv_ref,
                    a_ref, dq_ref, dk_ref, dv_ref, q_ref, k_ref, v_ref):
    h = _rms(x_ref[...], g_ref[...]).astype(BF16)
    c0 = 0
    u = _dot(h, w_ref[:, c0:c0 + W_GLU])
    a_ref[...] = u[:, :CONV_DIM] * jax.nn.sigmoid(u[:, CONV_DIM:])
    c0 += W_GLU
    _mla_heads(_dot(h, w_ref[:, c0:c0 + W_SMALL]), cos_ref[...], sin_ref[...], gq_ref, gkv_ref,
               wqa_ref, wqb_ref, wk_ref, wv_ref, q_ref, k_ref, v_ref)
    c0 += W_SMALL
    dq_ref[...] = (_dot(h, w_ref[:, c0:c0 + W_DQK]) * (DIFF_HEAD ** -0.5 * LOG2E)).astype(BF16)
    c0 += W_DQK
    dk_ref[...] = _dot(h, w_ref[:, c0:c0 + W_DQK]).astype(BF16)
    c0 += W_DQK
    dv_ref[...] = _dot(h, w_ref[:, c0:c0 + W_DV]).astype(BF16)


def _in_proj(l, x2, cos_t, sin_t, layer_ops, tm, b, s):
    t = x2.shape[0]
    tiles_per_seq = s // tm
    row = lambda n: pl.BlockSpec((tm, n), lambda i: (i, 0))
    heads = pl.BlockSpec((1, MLA_HEADS, tm, LANES), lambda i: (i // tiles_per_seq, 0, i % tiles_per_seq, 0))
    g, w, gq, gkv, wqa, wqb, wk, wv = layer_ops
    lb = lambda a: _layer_block(a, l)
    return pl.pallas_call(
        _in_proj_kernel,
        grid=(t // tm,),
        in_specs=[row(D_MODEL), lb(g), lb(w), row(LANES), row(LANES), lb(gq), lb(gkv),
                  lb(wqa), lb(wqb), lb(wk), lb(wv)],
        out_specs=[row(CONV_DIM), row(W_DQK), row(W_DQK), row(W_DV), heads, heads, heads],
        out_shape=[jax.ShapeDtypeStruct((t, CONV_DIM), F32),
                   jax.ShapeDtypeStruct((t, W_DQK), BF16),
                   jax.ShapeDtypeStruct((t, W_DQK), BF16),
                   jax.ShapeDtypeStruct((t, W_DV), BF16)]
                  + [jax.ShapeDtypeStruct((b, MLA_HEADS, s, LANES), BF16)] * 3,
        compiler_params=_cparams("parallel"),
        name="in_proj",
    )(x2, g, w, cos_t, sin_t, gq, gkv, wqa, wqb, wk, wv)


def _score_tile(q, k, bias, s_ref, rmax_ref):
    s = _dot_nt(q, k) + bias
    s_ref[...] = s
    chunks = [s[:, c:c + LANES] for c in range(0, s.shape[1], LANES)]
    row_max = jnp.max(functools.reduce(jnp.maximum, chunks), axis=1, keepdims=True)
    rmax_ref[...] = jnp.broadcast_to(row_max, rmax_ref.shape)


def _softmax_update(s_ref, rmax_ref, v, m_ref, acc_ref):
    m_old = m_ref[...]
    m_new = jnp.maximum(m_old, rmax_ref[...])
    alpha = jnp.exp2(m_old - m_new)
    p = jnp.concatenate([jnp.exp2(s_ref[:, c:c + LANES] - m_new) for c in range(0, s_ref.shape[1], LANES)], axis=1)
    pv = _dot(p.astype(BF16), v)
    acc_ref[...] = jnp.concatenate([alpha] * (acc_ref.shape[1] // LANES), axis=1) * acc_ref[...] + pv
    m_ref[...] = m_new


def _causal_tiles(nq):
    return [(qi, kj) for qi in range(nq) for kj in range(qi + 1)]


def _tile_tables(nq):
    tiles = np.asarray(_causal_tiles(nq), dtype=np.int32)
    return jnp.asarray(tiles[:, 0]), jnp.asarray(tiles[:, 1])


def _causal_flash(streams, qtab, ktab, n_tiles, tq, bias_ref, s_buf, rmax_buf, unroll):
    n_kinds = bias_ref.shape[0]

    def blk(ref, j):
        return ref[pl.ds(pl.multiple_of(j * tq, tq), tq), :]

    def scores(t, dst):
        qi, kj = qtab[t], ktab[t]
        bias = bias_ref[jnp.minimum(qi - kj, n_kinds - 1)]
        for si, (q_ref, k_ref, _, _, _) in enumerate(streams):
            _score_tile(blk(q_ref, qi), blk(k_ref, kj), bias, s_buf.at[dst, si], rmax_buf.at[dst, si])

    def update(t, src):
        qi, kj = qtab[t], ktab[t]
        for si, (_, _, v_ref, m_ref, acc_ref) in enumerate(streams):
            _softmax_update(s_buf.at[src, si], rmax_buf.at[src, si], blk(v_ref, kj), m_ref.at[qi], acc_ref.at[qi])

    def group(jj, carry):
        for u in range(unroll):
            t = unroll * jj + u
            scores(t + 1, (u + 1) % 2)
            update(t, u % 2)
        return carry

    for _, _, _, m_ref, acc_ref in streams:
        m_ref[...] = jnp.full(m_ref.shape, NEG_BIG, F32)
        acc_ref[...] = jnp.zeros(acc_ref.shape, F32)
    scores(0, 0)
    assert unroll % 2 == 0
    n_groups = (n_tiles - 1) // unroll
    lax.fori_loop(0, n_groups, group, 0)
    for t in range(n_groups * unroll, n_tiles - 1):
        scores(t + 1, (t + 1) % 2)
        update(t, t % 2)
    update(n_tiles - 1, (n_tiles - 1) % 2)


def _mla_attn_kernel(qtab, ktab, q_ref, k_ref, v_ref, o_ref, m_ref, acc_ref, s_buf, rmax_buf, mask_ref,
                     *, tq, nq):
    @pl.when((pl.program_id(0) == 0) & (pl.program_id(1) == 0))
    def _():
        row = lax.broadcasted_iota(jnp.int32, (tq, tq), 0)
        col = lax.broadcasted_iota(jnp.int32, (tq, tq), 1)
        mask_ref[0] = jnp.where(row >= col, 0.0, NEG_BIG).astype(F32)
        mask_ref[1] = jnp.zeros((tq, tq), F32)

    streams = [(q_ref.at[0, hh], k_ref.at[0, hh], v_ref.at[0, hh], m_ref.at[hh], acc_ref.at[hh])
               for hh in range(2)]
    _causal_flash(streams, qtab, ktab, nq * (nq + 1) // 2, tq, mask_ref, s_buf, rmax_buf, MLA_FLASH_UNROLL)
    low = lax.broadcasted_iota(jnp.int32, (1, LANES), 1) < MLA_V
    for qi in range(nq):
        even, odd = acc_ref[0, qi], acc_ref[1, qi]
        num = jnp.where(low, even, odd)
        den = pltpu.roll(jnp.where(low, odd, even), MLA_V, axis=1)
        o_ref[0, qi * tq:(qi + 1) * tq, :] = (num / den).astype(BF16)


def _mla_attn(q, k, v, tq):
    b, nh, s, _ = q.shape
    nq = s // tq
    qtab, ktab = _tile_tables(nq)
    smem = pl.BlockSpec(memory_space=pltpu.SMEM)
    seq = pl.BlockSpec((1, 2, s, LANES), lambda bi, hp: (bi, hp, 0, 0))
    return pl.pallas_call(
        functools.partial(_mla_attn_kernel, tq=tq, nq=nq),
        grid=(b, nh // 2),
        in_specs=[smem, smem, seq, seq, seq],
        out_specs=pl.BlockSpec((1, s, LANES), lambda bi, hp: (bi, 0, hp)),
        out_shape=jax.ShapeDtypeStruct((b, s, nh * MLA_V), BF16),
        scratch_shapes=[pltpu.VMEM((2, nq, tq, LANES), F32), pltpu.VMEM((2, nq, tq, LANES), F32),
                        pltpu.VMEM((2, 2, tq, tq), F32), pltpu.VMEM((2, 2, tq, LANES), F32),
                        pltpu.VMEM((2, tq, tq), F32)],
        compiler_params=_cparams("arbitrary", "arbitrary"),
        name="mla_attn",
    )(qtab, ktab, q, k, v)


def _bucket_upper_bounds():
    n = np.arange(0, 4 * REL_MAX_DIST, dtype=np.int32)
    max_exact = REL_BUCKETS // 2
    nf = np.maximum(n, 1).astype(np.float32)
    large = max_exact + (np.log(nf / np.float32(max_exact)) / np.float32(math.log(REL_MAX_DIST / max_exact))
                         * np.float32(REL_BUCKETS - max_exact)).astype(np.int32)
    large = np.minimum(large, REL_BUCKETS - 1)
    bucket = np.where(n < max_exact, n, large)
    assert np.all(np.diff(bucket) >= 0) and bucket[-1] == REL_BUCKETS - 1
    return [int(np.max(n[bucket == b])) for b in range(REL_BUCKETS - 1)]


def _bias_tile_kernel(tbl_ref, o_ref, *, tq, bounds):
    h = pl.program_id(0)
    kind = pl.program_id(1)
    row = lax.broadcasted_iota(jnp.int32, (tq, tq), 0)
    col = lax.broadcasted_iota(jnp.int32, (tq, tq), 1)
    n = row - col + kind * tq
    far = tbl_ref[(REL_BUCKETS - 1) * DIFF_HEADS + h]
    r = jnp.zeros((tq, tq), F32)
    for bkt in range(REL_BUCKETS - 2, -1, -1):
        r = jnp.where(n <= bounds[bkt], (tbl_ref[bkt * DIFF_HEADS + h] - far) * LOG2E, r)
    o_ref[0, 0] = jnp.where(n >= 0, r, NEG_BIG)


def _bias_tiles(rel_bias, tq):
    bounds = _bucket_upper_bounds()
    assert bounds[-1] < tq, "bias must be constant from the second key block before the diagonal on"
    return pl.pallas_call(
        functools.partial(_bias_tile_kernel, tq=tq, bounds=bounds),
        grid=(DIFF_HEADS, 3),
        in_specs=[pl.BlockSpec(memory_space=pltpu.SMEM)],
        out_specs=pl.BlockSpec((1, 1, tq, tq), lambda h, kd: (h, kd, 0, 0)),
        out_shape=jax.ShapeDtypeStruct((DIFF_HEADS, 3, tq, tq), F32),
        compiler_params=_cparams("parallel", "parallel"),
        name="rel_bias_tiles",
    )(rel_bias.reshape(-1))


def _diff_attn_kernel(qtab, ktab, q_ref, k_ref, v_ref, bias_ref, lq1_ref, lk1_ref, lq2_ref, lk2_ref, gsub_ref,
                      o_ref, m_ref, acc_ref, s_buf, rmax_buf, qm_ref, vx_ref, *, tq, nq, lam_init):
    lam = (jnp.exp(jnp.sum(lq1_ref[...] * lk1_ref[...], axis=1, keepdims=True))
           - jnp.exp(jnp.sum(lq2_ref[...] * lk2_ref[...], axis=1, keepdims=True)) + lam_init)
    q = q_ref[0]
    lane = lax.broadcasted_iota(jnp.int32, (1, LANES), 1)
    zero = jnp.zeros_like(q)
    qm_ref[0] = jnp.where(lane < DIFF_HEAD, q, zero)
    qm_ref[1] = jnp.where(lane >= DIFF_HEAD, q, zero)
    vx_ref[:, :DIFF_V] = v_ref[0]
    vx_ref[:, DIFF_V:] = jnp.ones((vx_ref.shape[0], LANES), BF16)
    streams = [(qm_ref.at[mp], k_ref.at[0], vx_ref, m_ref.at[mp], acc_ref.at[mp]) for mp in range(2)]
    _causal_flash(streams, qtab, ktab, nq * (nq + 1) // 2, tq, bias_ref.at[0], s_buf, rmax_buf, DIFF_FLASH_UNROLL)
    for qi in range(nq):
        outs = []
        for mp in range(2):
            acc = acc_ref[mp, qi]
            outs.append(acc[:, :DIFF_V] / acc[:, DIFF_V:])
        o = outs[0] - lam * outs[1]
        o_ref[0, qi * tq:(qi + 1) * tq, :] = (_rms(o, gsub_ref[...]) * (1.0 - lam_init)).astype(BF16)


def _diff_attn(l, dq, dk, dv, bias_tiles, layer_ops, lam_init, tq):
    b, s, _ = dq.shape
    nq = s // tq
    qtab, ktab = _tile_tables(nq)
    smem = pl.BlockSpec(memory_space=pltpu.SMEM)
    seq = pl.BlockSpec((1, s, LANES), lambda bi, h: (bi, 0, h))
    n_kinds = bias_tiles.shape[1]
    return pl.pallas_call(
        functools.partial(_diff_attn_kernel, tq=tq, nq=nq, lam_init=lam_init),
        grid=(b, DIFF_HEADS),
        in_specs=[smem, smem, seq, seq, seq,
                  pl.BlockSpec((1, n_kinds, tq, tq), lambda bi, h: (h, 0, 0, 0))]
                 + [_layer_block(a, l) for a in layer_ops],
        out_specs=seq,
        out_shape=jax.ShapeDtypeStruct((b, s, W_DV), BF16),
        scratch_shapes=[pltpu.VMEM((2, nq, tq, LANES), F32), pltpu.VMEM((2, nq, tq, 2 * LANES), F32),
                        pltpu.VMEM((2, 2, tq, tq), F32), pltpu.VMEM((2, 2, tq, LANES), F32),
                        pltpu.VMEM((2, s, LANES), BF16), pltpu.VMEM((s, 2 * LANES), BF16)],
        compiler_params=_cparams("parallel", "parallel"),
        name="diff_attn",
    )(qtab, ktab, dq, dk, dv, bias_tiles, *layer_ops)


def _conv_ln_silu(a, halo, w_ref, b_ref, lng_ref, lnb_ref, buf_ref, shift_ref, tm):
    buf_ref[0:CONV_HALO, :] = halo
    buf_ref[CONV_HALO:, :] = a
    n_rows = tm + CONV_HALO - SUBLANES
    for r in range(1, SUBLANES):
        shift_ref[r - 1] = buf_ref[r:r + n_rows, :]
    base = CONV_HALO - (CONV_WIDTH - 1)
    y = jnp.zeros((tm, CONV_DIM), F32) + b_ref[...]
    for j in range(CONV_WIDTH):
        grp, r = divmod(base + j, SUBLANES)
        lo = grp * SUBLANES
        tap = buf_ref[lo:lo + tm, :] if r == 0 else shift_ref[r - 1, lo:lo + tm, :]
        y = y + w_ref[j:j + 1, :] * tap
    mu = jnp.mean(y, axis=-1, keepdims=True)
    var = jnp.mean(jnp.square(y - mu), axis=-1, keepdims=True)
    z = (y - mu) * lax.rsqrt(var + EPS) * lng_ref[...] + lnb_ref[...]
    return (z * jax.nn.sigmoid(z)).astype(BF16)


def _merge_kernel(x_ref, g_ref, wg_ref, bg_ref, a_ref, halo_ref, cw_ref, cb_ref, lng_ref, lnb_ref,
                  yb_ref, yc_ref, wa_ref, wb_ref, wc_ref, wo_ref, o_ref, buf_ref, shift_ref, *, tm, tiles_per_seq):
    x = x_ref[...]
    h = _rms(x, g_ref[...]).astype(BF16)
    halo = halo_ref[...]
    starts_seq = (pl.program_id(0) % tiles_per_seq) == 0
    ya = _conv_ln_silu(a_ref[...], jnp.where(starts_seq, jnp.zeros_like(halo), halo), cw_ref, cb_ref, lng_ref,
                       lnb_ref, buf_ref, shift_ref, tm)
    merged = None
    for br, (y, w_ref) in enumerate(((ya, wa_ref), (yb_ref[...], wb_ref), (yc_ref[...], wc_ref))):
        sl = slice(br * D_MODEL, (br + 1) * D_MODEL)
        gate = jax.nn.sigmoid(_dot(h, wg_ref[:, sl]) + bg_ref[:, sl])
        term = gate * _dot(y, w_ref[...])
        merged = term if merged is None else merged + term
    o_ref[...] = x + _dot(merged.astype(BF16), wo_ref[...])


def _merge(l, x2, a2, yb, yc, layer_ops, tm, tiles_per_seq):
    t = x2.shape[0]
    per = tm // CONV_HALO
    row = lambda w: pl.BlockSpec((tm, w), lambda i: (i, 0))
    halo = pl.BlockSpec((CONV_HALO, CONV_DIM), lambda i: (jnp.maximum(i * per - 1, 0), 0))
    g, wg, bg, cw, cb, lng, lnb, wa, wb, wc, wo = layer_ops
    lb = lambda a: _layer_block(a, l)
    return pl.pallas_call(
        functools.partial(_merge_kernel, tm=tm, tiles_per_seq=tiles_per_seq),
        grid=(t // tm,),
        in_specs=[row(D_MODEL), lb(g), lb(wg), lb(bg), row(CONV_DIM), halo, lb(cw), lb(cb), lb(lng),
                  lb(lnb), row(MLA_HEADS * MLA_V), row(W_DV), lb(wa), lb(wb), lb(wc), lb(wo)],
        out_specs=row(D_MODEL),
        out_shape=jax.ShapeDtypeStruct((t, D_MODEL), F32),
        scratch_shapes=[pltpu.VMEM((tm + CONV_HALO, CONV_DIM), F32),
                        pltpu.VMEM((SUBLANES - 1, tm + CONV_HALO - SUBLANES, CONV_DIM), F32)],
        compiler_params=_cparams("parallel"),
        name="merge",
    )(x2, g, wg, bg, a2, a2, cw, cb, lng, lnb, yb, yc, wa, wb, wc, wo)


def _ffn_kernel(x_ref, g_ref, wv_ref, wg_ref, cw_ref, cb_ref, wd_ref, gf_ref, o_ref,
                h_ref, buf_ref, carry_ref, *, tm, tiles_per_seq, final_norm):
    i = pl.program_id(0)
    x = x_ref[...]
    h_ref[...] = _rms(x, g_ref[...]).astype(BF16)
    o_ref[...] = x
    pad = SUBLANES

    @pl.when((i % tiles_per_seq) == 0)
    def _():
        carry_ref[...] = jnp.zeros_like(carry_ref)

    def up(c, slot):
        h = h_ref[...]
        buf_ref[slot, 0, pad:, :] = _dot(h, wv_ref[c])
        buf_ref[slot, 1, pad:, :] = _dot(h, wg_ref[c])

    def conv3(c, slot, half):
        buf = buf_ref.at[slot, half]
        buf[0:pad, :] = carry_ref[c, half]
        carry_ref[c, half] = buf[tm:tm + pad, :]
        w = cw_ref[c, half]
        return (w[0:1, :] * buf[pad - 2:pad - 2 + tm, :] + w[1:2, :] * buf[pad - 1:pad - 1 + tm, :]
                + w[2:3, :] * buf[pad:pad + tm, :] + cb_ref[c, half])

    def down(c, slot):
        uv = conv3(c, slot, 0)
        ug = conv3(c, slot, 1)
        act = (uv * jax.nn.sigmoid(uv) * ug).astype(BF16)
        o_ref[...] += _dot(act, wd_ref[c])

    def pair(cc, carry):
        c = 2 * cc
        up(c + 1, 1)
        down(c, 0)
        up(c + 2, 0)
        down(c + 1, 1)
        return carry

    up(0, 0)
    n_pairs = (N_FF_CHUNK - 1) // 2
    lax.fori_loop(0, n_pairs, pair, 0)
    if (N_FF_CHUNK - 1) % 2 == 1:
        up(N_FF_CHUNK - 1, 1)
        down(N_FF_CHUNK - 2, 0)
        down(N_FF_CHUNK - 1, 1)
    else:
        down(N_FF_CHUNK - 1, 0)
    if final_norm:
        o_ref[...] = _rms(o_ref[...], gf_ref[...])


def _ffn(l, x2, layer_ops, gf, tm, tiles_per_seq, final_norm):
    t = x2.shape[0]
    row = pl.BlockSpec((tm, D_MODEL), lambda i: (i, 0))
    return pl.pallas_call(
        functools.partial(_ffn_kernel, tm=tm, tiles_per_seq=tiles_per_seq, final_norm=final_norm),
        grid=(t // tm,),
        in_specs=[row] + [_layer_block(a, l) for a in layer_ops]
                 + [pl.BlockSpec(gf.shape, lambda i: (0, 0), pipeline_mode=pl.Buffered(1))],
        out_specs=row,
        out_shape=jax.ShapeDtypeStruct((t, D_MODEL), F32),
        scratch_shapes=[pltpu.VMEM((tm, D_MODEL), BF16),
                        pltpu.VMEM((2, 2, tm + SUBLANES, FF_CHUNK), F32),
                        pltpu.VMEM((N_FF_CHUNK, 2, SUBLANES, FF_CHUNK), F32)],
        compiler_params=_cparams("arbitrary"),
        name="ffn",
    )(x2, *layer_ops, gf)


def _rot_cols(w):
    half = w.shape[-1] // 2
    return jnp.concatenate([-w[..., half:], w[..., :half]], axis=-1)


def _prep_in_proj_weight(w_in):
    cuts = np.cumsum([W_GLU, MLA_Q_RANK, MLA_KV_RANK, MLA_ROPE, W_DQK, W_DQK, W_DV])
    glu, cq, ckv, kr, dq, dk, dv, gates = jnp.split(w_in, cuts, axis=-1)
    z = lambda n: jnp.zeros(w_in.shape[:-1] + (n,), w_in.dtype)
    tail = LANES - MLA_NOPE - MLA_ROPE
    w = jnp.concatenate([glu, cq, ckv, z(MLA_NOPE), kr, z(tail), z(MLA_NOPE), _rot_cols(kr), z(tail), dq, dk, dv],
                        axis=-1)
    return w.astype(BF16), gates.astype(BF16)


def _prep_mla_weights(w_uq, w_ukv):
    dqk = MLA_NOPE + MLA_ROPE
    wq = w_uq.reshape(w_uq.shape[:-1] + (MLA_HEADS, dqk))
    pad = jnp.zeros(wq.shape[:-1] + (LANES - dqk,), wq.dtype)
    wqa = jnp.concatenate([wq, pad], axis=-1)
    wqb = jnp.concatenate([jnp.zeros_like(wq[..., :MLA_NOPE]), _rot_cols(wq[..., MLA_NOPE:]), pad], axis=-1)
    wkv = w_ukv.reshape(w_ukv.shape[:-1] + (MLA_HEADS, MLA_NOPE + MLA_V))
    zk = jnp.zeros(wkv.shape[:-1] + (LANES - MLA_NOPE,), wkv.dtype)
    wk = jnp.concatenate([wkv[..., :MLA_NOPE], zk], axis=-1)
    zv = jnp.zeros(wkv.shape[:-1] + (LANES - MLA_V,), wkv.dtype)
    v_even = jnp.concatenate([wkv[..., MLA_NOPE:], zv], axis=-1)
    v_odd = jnp.concatenate([zv, wkv[..., MLA_NOPE:]], axis=-1)
    odd = (jnp.arange(MLA_HEADS) % 2 == 1)[:, None]
    wv = jnp.where(odd, v_odd, v_even)
    flat = lambda a: a.reshape(a.shape[:-2] + (MLA_HEADS * LANES,)).astype(BF16)
    return flat(wqa), flat(wqb), flat(wk), flat(wv)


def _rope_tables(positions):
    half = MLA_ROPE // 2
    freqs = ROPE_THETA ** (-jnp.arange(half, dtype=F32) / half)
    ang = positions.astype(F32)[..., None] * freqs
    cos, sin = jnp.cos(ang), jnp.sin(ang)
    b, s = positions.shape
    ones = jnp.ones((b, s, MLA_NOPE), F32)
    zeros_n = jnp.zeros((b, s, MLA_NOPE), F32)
    zeros_t = jnp.zeros((b, s, LANES - MLA_NOPE - MLA_ROPE), F32)
    cos_t = jnp.concatenate([ones, cos, cos, zeros_t], axis=-1)
    sin_t = jnp.concatenate([zeros_n, sin, sin, zeros_t], axis=-1)
    return cos_t, sin_t


def _chunk_cols(w):
    return jnp.moveaxis(w.reshape(w.shape[:-1] + (-1, FF_CHUNK)), -2, -3)


def kernel(x, positions, rel_bias, norm_mix, w_in, gate_bias, conv_w, conv_b, conv_ln_g, conv_ln_b, w_conv_out,
           mla_q_norm, w_uq, mla_kv_norm, w_ukv, w_mla_out, diff_lam_q1, diff_lam_k1, diff_lam_q2, diff_lam_k2,
           diff_sub_norm, w_diff_out, w_out, norm_ffn, w_up, ffn_conv_w, ffn_conv_b, w_down, norm_final):
    b, s, d = x.shape
    depth = w_in.shape[0]
    assert d == D_MODEL
    tm = min(512, s)
    tm_ffn = min(1024, s)
    tq = min(512, s)
    assert s % tm == 0 and s % tm_ffn == 0 and s % tq == 0 and tm % CONV_HALO == 0
    t = b * s
    vec = lambda a: a.reshape(depth, 1, -1)
    bf = lambda a: a.astype(BF16)

    cos_t, sin_t = (tbl.reshape(t, LANES) for tbl in _rope_tables(positions))
    bias_tiles = _bias_tiles(rel_bias, tq)

    w_in_p, w_gate = _prep_in_proj_weight(w_in)
    wqa, wqb, wk, wv = _prep_mla_weights(w_uq, w_ukv)
    in_proj_ops = (vec(norm_mix), w_in_p, vec(mla_q_norm), vec(mla_kv_norm), wqa, wqb, wk, wv)
    diff_ops = (vec(diff_lam_q1), vec(diff_lam_k1), vec(diff_lam_q2), vec(diff_lam_k2), vec(diff_sub_norm))
    merge_ops = (vec(norm_mix), w_gate, vec(gate_bias), conv_w, vec(conv_b), vec(conv_ln_g), vec(conv_ln_b),
                 bf(w_conv_out), bf(w_mla_out), bf(w_diff_out), bf(w_out))
    w_up_b = bf(w_up)
    ffn_ops = (vec(norm_ffn), _chunk_cols(w_up_b[..., :D_FF]), _chunk_cols(w_up_b[..., D_FF:]),
               ffn_conv_w.reshape(depth, FFN_CONV, 2, N_FF_CHUNK, FF_CHUNK).transpose(0, 3, 2, 1, 4),
               ffn_conv_b.reshape(depth, 2, N_FF_CHUNK, 1, FF_CHUNK).transpose(0, 2, 1, 3, 4),
               bf(w_down).reshape(depth, N_FF_CHUNK, FF_CHUNK, D_MODEL))

    x2 = x.reshape(t, d)
    for l in range(depth):
        a, dq, dk, dv, q, k, v = _in_proj(l, x2, cos_t, sin_t, in_proj_ops, tm, b, s)
        y_b = _mla_attn(q, k, v, tq)
        lam_init = 0.8 - 0.6 * math.exp(-0.3 * l)
        y_c = _diff_attn(l, dq.reshape(b, s, W_DQK), dk.reshape(b, s, W_DQK), dv.reshape(b, s, W_DV), bias_tiles,
                         diff_ops, lam_init, tq)
        x2 = _merge(l, x2, a, y_b.reshape(t, MLA_HEADS * MLA_V), y_c.reshape(t, W_DV), merge_ops, tm, s // tm)
        x2 = _ffn(l, x2, ffn_ops, norm_final.reshape(1, -1), tm_ffn, s // tm_ffn, l == depth - 1)
    return x2.reshape(b, s, d)
```
